```python
import math
import jax, jax.numpy as jnp
from jax import lax
import numpy as np

D_MODEL = 1024
BATCH = 16
SEQ = 4096
DEPTH = 4

HEAD_DIM = 64
N_HEADS_DIL = 8
DIL_WIDTH = N_HEADS_DIL * HEAD_DIM
N_HEADS_DIFF = 4
DIFF_WIDTH = N_HEADS_DIFF * 2 * HEAD_DIM
MIX_WIDTH = DIL_WIDTH + DIFF_WIDTH
IN_WIDTH = 3 * MIX_WIDTH
DILATIONS = ((128, 1), (512, 4), (2048, 16))
BAND = 128
N_BUCKETS = 32
MAX_DISTANCE = 2048
N_BIAS_HEADS = N_HEADS_DIL + N_HEADS_DIFF
D_FF = -(-8 * D_MODEL // (3 * 256)) * 256
Q_BLOCK = 128
EPS = 1e-6
SUBLN_EPS = 1e-5
NEG = -1e30

kernel_name = "hybrid_dilated_diffattn_block"


def rmsnorm(x, g, eps=EPS):
    xf = x.astype(jnp.float32)
    y = xf * lax.rsqrt(jnp.mean(xf * xf, axis=-1, keepdims=True) + eps)
    return (y * g.astype(jnp.float32)).astype(x.dtype)


def rel_bucket(dist):
    max_exact = N_BUCKETS // 2
    n = jnp.maximum(dist, 0)
    nf = jnp.maximum(n, 1).astype(jnp.float32)
    large = max_exact + (jnp.log(nf / max_exact) / math.log(MAX_DISTANCE / max_exact)
                         * (N_BUCKETS - max_exact)).astype(jnp.int32)
    large = jnp.minimum(large, N_BUCKETS - 1)
    return jnp.where(n < max_exact, n, large)


def dilated_branch(q, k, v, bias_tab, window, dil):
    B, S, H, hd = q.shape
    L = S // dil
    nb = -(-L // BAND)
    Lp = nb * BAND
    n_keys = window // dil

    def strided(t):
        t = t.reshape(B, L, dil, H, hd).transpose(0, 2, 1, 3, 4)
        t = jnp.pad(t, ((0, 0), (0, 0), (0, Lp - L), (0, 0), (0, 0)))
        return t.reshape(B, dil, nb, BAND, H, hd)

    def band(t):
        prev = jnp.pad(t, ((0, 0), (0, 0), (1, 0), (0, 0), (0, 0), (0, 0)))[:, :, :-1]
        return jnp.concatenate([prev, t], axis=3)

    qb = strided(q)
    kk = band(strided(k))
    vv = band(strided(v))
    scale = HEAD_DIM ** -0.5
    s = jnp.einsum('brnqhd,brnkhd->brnhqk', qb, kk,
                   preferred_element_type=jnp.float32) * scale
    i = jnp.arange(BAND)[:, None]
    j = jnp.arange(2 * BAND)[None, :]
    m = i + BAND - j
    bias = bias_tab[rel_bucket(m * dil)].transpose(2, 0, 1).astype(jnp.float32)
    valid = (m >= 0) & (m <= n_keys)
    blk = jnp.arange(nb)[:, None, None]
    mask = valid[None] & ((blk > 0) | (j >= BAND)[None])
    s = jnp.where(mask[None, None, :, None], s + bias, NEG)
    lse = jax.nn.logsumexp(s, axis=-1)
    p = jnp.exp(s - lse[..., None])
    o = jnp.einsum('brnhqk,brnkhd->brnqhd', p.astype(v.dtype), vv)
    o = o.reshape(B, dil, Lp, H, hd)[:, :, :L].transpose(0, 2, 1, 3, 4).reshape(B, S, H, hd)
    lse = lse.transpose(0, 1, 2, 4, 3).reshape(B, dil, Lp, H)[:, :, :L]
    lse = lse.transpose(0, 2, 1, 3).reshape(B, S, H)
    return o, lse


def dilated_attention(q, k, v, bias_tab):
    outs, lses = [], []
    for window, dil in DILATIONS:
        o, l = dilated_branch(q, k, v, bias_tab, window, dil)
        outs.append(o)
        lses.append(l)
    w = jax.nn.softmax(jnp.stack(lses, axis=0), axis=0)
    o = jnp.sum(w[..., None] * jnp.stack(outs, axis=0).astype(jnp.float32), axis=0)
    return o.astype(q.dtype)


def diff_attention(q, k, v, bias_tab, lam, subln_g, lam_init):
    B, S, H, _, hd = q.shape
    nqb = S // Q_BLOCK
    scale = HEAD_DIM ** -0.5
    qblocks = q.reshape(B, nqb, Q_BLOCK, H, 2, hd).transpose(1, 0, 2, 3, 4, 5)
    kpos = jnp.arange(S)

    def block(args):
        qb, b_idx = args
        qpos = b_idx * Q_BLOCK + jnp.arange(Q_BLOCK)
        dist = qpos[:, None] - kpos[None, :]
        bias = bias_tab[rel_bucket(dist)].transpose(2, 0, 1).astype(jnp.float32)
        s = jnp.einsum('bqhcd,bkhcd->bhcqk', qb, k,
                       preferred_element_type=jnp.float32) * scale
        s = jnp.where(dist >= 0, s + bias[None, :, None], NEG)
        p = jax.nn.softmax(s, axis=-1)
        a = p[:, :, 0] - lam * p[:, :, 1]
        return jnp.einsum('bhqk,bkhd->bqhd', a.astype(v.dtype), v)

    out = lax.map(block, (qblocks, jnp.arange(nqb)))
    out = out.transpose(1, 0, 2, 3, 4).reshape(B, S, H, 2 * hd)
    return rmsnorm(out, subln_g, SUBLN_EPS) * (1.0 - lam_init)


def setup_inputs(seed: int = 0) -> dict:
    key = jax.random.key(seed)
    ks = jax.random.split(key, 12)
    f32 = jnp.float32
    x = jax.random.normal(ks[0], (BATCH, SEQ, D_MODEL), f32)
    g_attn = 1.0 + 0.01 * jax.random.normal(ks[1], (DEPTH, D_MODEL), f32)
    w_in = jax.random.normal(ks[2], (DEPTH, D_MODEL, IN_WIDTH), f32) * D_MODEL ** -0.5
    w_out = jax.random.normal(ks[3], (DEPTH, MIX_WIDTH, D_MODEL), f32) * MIX_WIDTH ** -0.5
    rel_bias = 0.1 * jax.random.normal(ks[4], (N_BUCKETS, N_BIAS_HEADS), f32)
    lambda_qk = 0.1 * jax.random.normal(ks[5], (DEPTH, 4, HEAD_DIM), f32)
    subln_g = 1.0 + 0.01 * jax.random.normal(ks[6], (DEPTH, 2 * HEAD_DIM), f32)
    g_ffn = 1.0 + 0.01 * jax.random.normal(ks[7], (DEPTH, D_MODEL), f32)
    w_gate_up = jax.random.normal(ks[8], (DEPTH, D_MODEL, 2 * D_FF), f32) * D_MODEL ** -0.5
    w_down = jax.random.normal(ks[9], (DEPTH, D_FF, D_MODEL), f32) * D_FF ** -0.5
    g_final = 1.0 + 0.01 * jax.random.normal(ks[10], (D_MODEL,), f32)
    return {"x": x, "g_attn": g_attn, "w_in": w_in, "w_out": w_out,
            "rel_bias": rel_bias, "lambda_qk": lambda_qk, "subln_g": subln_g,
            "g_ffn": g_ffn, "w_gate_up": w_gate_up, "w_down": w_down,
            "g_final": g_final}


def reference(x, g_attn, w_in, w_out, rel_bias, lambda_qk, subln_g, g_ffn,
              w_gate_up, w_down, g_final):
    B, S, _ = x.shape
    bias_dil = rel_bias[:, :N_HEADS_DIL]
    bias_diff = rel_bias[:, N_HEADS_DIL:]
    for l in range(DEPTH):
        h = rmsnorm(x, g_attn[l])
        proj = h @ w_in[l]
        qa, ka, va, qb, kb, vb = jnp.split(proj, 6, axis=-1)
        hs = (B, S, N_HEADS_DIL, HEAD_DIM)
        oa = dilated_attention(qa.reshape(hs), ka.reshape(hs), va.reshape(hs), bias_dil)
        lam_init = 0.8 - 0.6 * math.exp(-0.3 * l)
        lq = lambda_qk[l].astype(jnp.float32)
        lam = (jnp.exp(jnp.sum(lq[0] * lq[1])) - jnp.exp(jnp.sum(lq[2] * lq[3]))
               + lam_init)
        ds = (B, S, N_HEADS_DIFF, 2, HEAD_DIM)
        ob = diff_attention(qb.reshape(ds), kb.reshape(ds),
                            vb.reshape(B, S, N_HEADS_DIFF, 2 * HEAD_DIM),
                            bias_diff, lam, subln_g[l], lam_init)
        mix = jnp.concatenate([oa.reshape(B, S, DIL_WIDTH),
                               ob.reshape(B, S, DIFF_WIDTH).astype(oa.dtype)], axis=-1)
        x = x + mix @ w_out[l]
        h = rmsnorm(x, g_ffn[l])
        gate, up = jnp.split(h @ w_gate_up[l], 2, axis=-1)
        x = x + (jax.nn.silu(gate) * up) @ w_down[l]
    return rmsnorm(x, g_final)
```

```python
import functools
import math

import jax
import jax.numpy as jnp
from jax import lax
from jax.experimental import pallas as pl
from jax.experimental.pallas import tpu as pltpu

F32 = jnp.float32
BF16 = jnp.bfloat16

HEAD_DIM = 64
N_HEADS_DIL = 8
N_HEADS_DIFF = 4
DIL_WIDTH = N_HEADS_DIL * HEAD_DIM
DIFF_WIDTH = N_HEADS_DIFF * 2 * HEAD_DIM
MIX_WIDTH = DIL_WIDTH + DIFF_WIDTH
DILATIONS = ((128, 1), (512, 4), (2048, 16))
BAND = 128
N_BUCKETS = 32
MAX_DISTANCE = 2048
EPS = 1e-6
SUBLN_EPS = 1e-5
NEG = -1e30

LANES = 128
VMEM_LIMIT = 56 * 1024 * 1024
ROW_TILE = 512
DIFF_TILE = 256
FF_CHUNK = 768
STAT_L = N_HEADS_DIL


def _nt_dot(a, b):
    return lax.dot_general(a, b, (((1,), (1,)), ((), ())), preferred_element_type=F32)


def _dot(a, b):
    return jnp.dot(a, b, preferred_element_type=F32)


def _rms(x, g, eps):
    return (x * lax.rsqrt(jnp.mean(x * x, axis=-1, keepdims=True) + eps)) * g


def _rel_bucket(dist):
    max_exact = N_BUCKETS // 2
    n = jnp.maximum(dist, 0)
    nf = jnp.maximum(n, 1).astype(F32)
    large = max_exact + (jnp.log(nf / max_exact) / math.log(MAX_DISTANCE / max_exact)
                         * (N_BUCKETS - max_exact)).astype(jnp.int32)
    large = jnp.minimum(large, N_BUCKETS - 1)
    return jnp.where(n < max_exact, n, large)


def _bias_kernel(tab_ref, bkt_ref, o_ref, *, head0):
    h = pl.program_id(0) + head0
    bkt = bkt_ref[0]
    out = jnp.full(bkt.shape, NEG, F32)
    for b in range(N_BUCKETS):
        out = jnp.where(bkt == b, tab_ref[b, h], out)
    o_ref[0, 0] = out


def _bias_call(rel_bias, buckets, head0, n_heads):
    nt, r, c = buckets.shape
    return pl.pallas_call(
        functools.partial(_bias_kernel, head0=head0),
        grid=(n_heads, nt),
        in_specs=[pl.BlockSpec(memory_space=pltpu.SMEM),
                  pl.BlockSpec((1, r, c), lambda h, t: (t, 0, 0))],
        out_specs=pl.BlockSpec((1, 1, r, c), lambda h, t: (h, t, 0, 0)),
        out_shape=jax.ShapeDtypeStruct((n_heads, nt, r, c), F32),
        name="bias_tiles",
    )(rel_bias, buckets)


def _dilated_buckets():
    i = jnp.arange(BAND, dtype=jnp.int32)[:, None]
    j = jnp.arange(2 * BAND, dtype=jnp.int32)[None, :]
    m = i + BAND - j
    tiles = []
    for window, dil in DILATIONS:
        valid = (m >= 0) & (m <= window // dil)
        tiles.append(jnp.where(valid, _rel_bucket(m * dil), -1))
    return jnp.stack(tiles, axis=0)


def _diff_buckets(seq, tile):
    i = jnp.arange(tile, dtype=jnp.int32)[None, :, None]
    j = jnp.arange(tile, dtype=jnp.int32)[None, None, :]
    d = jnp.arange(seq // tile, dtype=jnp.int32)[:, None, None]
    dist = d * tile + i - j
    return jnp.where(dist >= 0, _rel_bucket(dist), -1)


def _inproj_kernel(x_ref, g_ref, w_ref, o_ref):
    h = _rms(x_ref[...], g_ref[...], EPS).astype(BF16)
    o_ref[...] = _dot(h, w_ref[...]).astype(BF16)


def _inproj_call(x2, g, w):
    m, d = x2.shape
    n = w.shape[1]
    return pl.pallas_call(
        _inproj_kernel,
        grid=(m // ROW_TILE,),
        in_specs=[pl.BlockSpec((ROW_TILE, d), lambda i: (i, 0)),
                  pl.BlockSpec((1, d), lambda i: (0, 0)),
                  pl.BlockSpec((d, n), lambda i: (0, 0))],
        out_specs=pl.BlockSpec((ROW_TILE, n), lambda i: (i, 0)),
        out_shape=jax.ShapeDtypeStruct((m, n), BF16),
        compiler_params=pltpu.CompilerParams(
            dimension_semantics=("parallel",), vmem_limit_bytes=VMEM_LIMIT),
        name="inproj",
    )(x2, g, w)


def _dil_kernel(*refs, first, last):
    q_ref, kp_ref, kc_ref, vp_ref, vc_ref, bias_ref = refs[:6]
    refs = refs[6:]
    if not first:
        acc_in_ref, st_in_ref = refs[:2]
        refs = refs[2:]
    if last:
        (o_ref,) = refs
    else:
        acc_out_ref, st_out_ref = refs

    neg_prev = jnp.where(pl.program_id(2) == 0, NEG, 0.0).astype(F32)
    q = q_ref[0]
    kp, kc, vp, vc = kp_ref[0], kc_ref[0], vp_ref[0], vc_ref[0]
    if not first:
        st_in = st_in_ref[0]
        acc_in = acc_in_ref[0]
    lane = lax.broadcasted_iota(jnp.int32, (BAND, LANES), 1)
    st_out = jnp.zeros((BAND, LANES), F32)
    outs = []
    for h in range(N_HEADS_DIL):
        c = slice(h * HEAD_DIM, (h + 1) * HEAD_DIM)
        qh = q[:, c]
        bias = bias_ref[0, h]
        sp = _nt_dot(qh, kp[:, c]) + (bias[:, :BAND] + neg_prev)
        sc = _nt_dot(qh, kc[:, c]) + bias[:, BAND:]
        m_new = jnp.maximum(jnp.max(sp, axis=-1, keepdims=True),
                            jnp.max(sc, axis=-1, keepdims=True))
        if not first:
            m_old = st_in[:, h:h + 1]
            l_old = st_in[:, STAT_L + h:STAT_L + h + 1]
            m_new = jnp.maximum(m_new, m_old)
        pp = jnp.exp(sp - m_new)
        pc = jnp.exp(sc - m_new)
        l_new = jnp.sum(pp, axis=-1, keepdims=True) + jnp.sum(pc, axis=-1, keepdims=True)
        pv = _dot(pp.astype(BF16), vp[:, c]) + _dot(pc.astype(BF16), vc[:, c])
        if not first:
            alpha = jnp.exp(m_old - m_new)
            l_new = l_new + alpha * l_old
            pv = pv + alpha * acc_in[:, c]
        if last:
            outs.append(pv / l_new)
        else:
            outs.append(pv)
            st_out = jnp.where(lane == h, m_new, st_out)
            st_out = jnp.where(lane == STAT_L + h, l_new, st_out)
    out = jnp.concatenate(outs, axis=-1)
    if last:
        o_ref[0] = out.astype(BF16)
    else:
        acc_out_ref[0] = out
        st_out_ref[0] = st_out


def _dil_call(proj, bias, state, *, dil, dil_index, first, last):
    b, s, w = proj.shape
    nblk = w // DIL_WIDTH
    l = s // dil
    nb = l // BAND
    pv = proj.reshape(b, l, dil * w)

    def col(which):
        return lambda bi, r, n: (bi, n, r * nblk + which)

    def col_prev(which):
        return lambda bi, r, n: (bi, jnp.maximum(n - 1, 0), r * nblk + which)

    blk = (1, BAND, DIL_WIDTH)
    in_specs = [pl.BlockSpec(blk, col(0)),
                pl.BlockSpec(blk, col_prev(1)), pl.BlockSpec(blk, col(1)),
                pl.BlockSpec(blk, col_prev(2)), pl.BlockSpec(blk, col(2)),
                pl.BlockSpec((1, N_HEADS_DIL, BAND, 2 * BAND),
                             lambda bi, r, n: (dil_index, 0, 0, 0))]
    args = [pv, pv, pv, pv, pv, bias]
    state_specs = [pl.BlockSpec((1, BAND, DIL_WIDTH), lambda bi, r, n: (bi, n, r)),
                   pl.BlockSpec((1, BAND, LANES), lambda bi, r, n: (bi, n, r))]
    if not first:
        acc, st = state
        in_specs += state_specs
        args += [acc.reshape(b, l, dil * DIL_WIDTH), st.reshape(b, l, dil * LANES)]
    if last:
        out_specs = state_specs[0]
        out_shape = jax.ShapeDtypeStruct((b, l, dil * DIL_WIDTH), BF16)
    else:
        out_specs = state_specs
        out_shape = [jax.ShapeDtypeStruct((b, l, dil * DIL_WIDTH), F32),
                     jax.ShapeDtypeStruct((b, l, dil * LANES), F32)]
    res = pl.pallas_call(
        functools.partial(_dil_kernel, first=first, last=last),
        grid=(b, dil, nb),
        in_specs=in_specs,
        out_specs=out_specs,
        out_shape=out_shape,
        compiler_params=pltpu.CompilerParams(
            dimension_semantics=("parallel", "parallel", "arbitrary"),
            vmem_limit_bytes=VMEM_LIMIT),
        name=f"dilated_d{dil}",
    )(*args)
    if last:
        return res.reshape(b, s, DIL_WIDTH)
    return res[0].reshape(b, s, DIL_WIDTH), res[1].reshape(b, s, LANES)


def _dilated_attention(proj, bias):
    state = None
    n = len(DILATIONS)
    for idx, (_, dil) in enumerate(DILATIONS):
        state = _dil_call(proj, bias, state, dil=dil, dil_index=idx,
                          first=idx == 0, last=idx == n - 1)
    return state


def _diff_kernel(lq_ref, q_ref, k_ref, v_ref, bias_ref, g_ref, o_ref,
                 m_ref, l_ref, acc_ref, *, lam_init):
    t = DIFF_TILE
    qi = pl.program_id(2)
    q = q_ref[0]
    lane = lax.broadcasted_iota(jnp.int32, q.shape, 1)
    zero = jnp.zeros_like(q)
    qq = jnp.concatenate([jnp.where(lane < HEAD_DIM, q, zero),
                          jnp.where(lane >= HEAD_DIM, q, zero)], axis=0)
    m_ref[...] = jnp.full(m_ref.shape, NEG, F32)
    l_ref[...] = jnp.zeros(l_ref.shape, F32)
    acc_ref[...] = jnp.zeros(acc_ref.shape, F32)

    def body(kj, carry):
        off = pl.multiple_of(kj * t, t)
        kb = k_ref[0, pl.ds(off, t), :]
        vb = v_ref[0, pl.ds(off, t), :]
        bias = bias_ref[0, qi - kj]
        s = _nt_dot(qq, kb)
        s = (s.reshape(2, t, t) + bias[None]).reshape(2 * t, t)
        m_prev = m_ref[...]
        m_next = jnp.maximum(m_prev, jnp.max(s, axis=-1, keepdims=True))
        p = jnp.exp(s - jnp.concatenate([m_next] * (t // LANES), axis=-1))
        alpha = jnp.exp(m_prev - m_next)
        l_ref[...] = alpha * l_ref[...] + jnp.sum(p, axis=-1, keepdims=True)
        acc_ref[...] = alpha * acc_ref[...] + _dot(p.astype(BF16), vb)
        m_ref[...] = m_next
        return carry

    lax.fori_loop(0, qi + 1, body, 0)

    lq = lq_ref[0]
    lam = (jnp.exp(jnp.sum(lq[0:1] * lq[1:2], axis=-1, keepdims=True))
           - jnp.exp(jnp.sum(lq[2:3] * lq[3:4], axis=-1, keepdims=True)) + lam_init)
    o = acc_ref[...] / l_ref[...]
    a = o[:t] - lam * o[t:]
    o_ref[0] = (_rms(a, g_ref[0], SUBLN_EPS) * (1.0 - lam_init)).astype(BF16)


def _diff_call(proj, bias, lq, g, *, lam_init):
    b, s, w = proj.shape
    t = DIFF_TILE
    hw = 2 * HEAD_DIM
    q0 = 3 * DIL_WIDTH // hw
    k0 = q0 + N_HEADS_DIFF
    v0 = k0 + N_HEADS_DIFF
    return pl.pallas_call(
        functools.partial(_diff_kernel, lam_init=lam_init),
        grid=(N_HEADS_DIFF, b, s // t),
        in_specs=[pl.BlockSpec((1, 4, HEAD_DIM), lambda h, bi, qi: (0, 0, 0)),
                  pl.BlockSpec((1, t, hw), lambda h, bi, qi: (bi, qi, q0 + h)),
                  pl.BlockSpec((1, s, hw), lambda h, bi, qi: (bi, 0, k0 + h)),
                  pl.BlockSpec((1, s, hw), lambda h, bi, qi: (bi, 0, v0 + h)),
                  pl.BlockSpec((1, s // t, t, t), lambda h, bi, qi: (h, 0, 0, 0)),
                  pl.BlockSpec((1, 1, hw), lambda h, bi, qi: (0, 0, 0))],
        out_specs=pl.BlockSpec((1, t, hw), lambda h, bi, qi: (bi, qi, h)),
        out_shape=jax.ShapeDtypeStruct((b, s, DIFF_WIDTH), BF16),
        scratch_shapes=[pltpu.VMEM((2 * t, LANES), F32),
                        pltpu.VMEM((2 * t, LANES), F32),
                        pltpu.VMEM((2 * t, hw), F32)],
        compiler_params=pltpu.CompilerParams(
            dimension_semantics=("parallel", "parallel", "arbitrary"),
            vmem_limit_bytes=VMEM_LIMIT),
        name="diff_attn",
    )(lq, proj, proj, proj, bias, g)


def _ff_chunks(d_ff):
    chunks, c0 = [], 0
    while c0 < d_ff:
        cw = min(FF_CHUNK, d_ff - c0)
        chunks.append((c0, cw))
        c0 += cw
    return chunks


def _mlp_kernel(*refs, final):
    x_ref, oa_ref, ob_ref, wo_ref, g_ref, wgu_ref, wd_ref = refs[:7]
    o_ref, act_ref = refs[-2:]
    d_ff = wd_ref.shape[0]
    x = (x_ref[...] + _dot(oa_ref[...], wo_ref[:DIL_WIDTH, :])
         + _dot(ob_ref[...], wo_ref[DIL_WIDTH:, :]))
    h = _rms(x, g_ref[...], EPS).astype(BF16)
    for c0, cw in _ff_chunks(d_ff):
        gate = _dot(h, wgu_ref[:, c0:c0 + cw])
        up = _dot(h, wgu_ref[:, d_ff + c0:d_ff + c0 + cw])
        act_ref[:, c0:c0 + cw] = ((gate * jax.nn.sigmoid(gate)) * up).astype(BF16)
    x = x + _dot(act_ref[...], wd_ref[...])
    if final:
        x = _rms(x, refs[7][...], EPS)
    o_ref[...] = x


def _mlp_call(x2, oa, ob, wo, g, wgu, wd, g_final):
    m, d = x2.shape
    d_ff = wd.shape[0]
    final = g_final is not None
    row = lambda i: (i, 0)
    fixed = lambda i: (0, 0)
    in_specs = [pl.BlockSpec((ROW_TILE, d), row),
                pl.BlockSpec((ROW_TILE, DIL_WIDTH), row),
                pl.BlockSpec((ROW_TILE, DIFF_WIDTH), row),
                pl.BlockSpec((MIX_WIDTH, d), fixed, pipeline_mode=pl.Buffered(1)),
                pl.BlockSpec((1, d), fixed),
                pl.BlockSpec((d, 2 * d_ff), fixed, pipeline_mode=pl.Buffered(1)),
                pl.BlockSpec((d_ff, d), fixed, pipeline_mode=pl.Buffered(1))]
    args = [x2, oa, ob, wo, g, wgu, wd]
    if final:
        in_specs.append(pl.BlockSpec((1, d), fixed))
        args.append(g_final)
    return pl.pallas_call(
        functools.partial(_mlp_kernel, final=final),
        grid=(m // ROW_TILE,),
        in_specs=in_specs,
        out_specs=pl.BlockSpec((ROW_TILE, d), row),
        out_shape=jax.ShapeDtypeStruct((m, d), F32),
        scratch_shapes=[pltpu.VMEM((ROW_TILE, d_ff), BF16)],
        compiler_params=pltpu.CompilerParams(
            dimension_semantics=("parallel",), vmem_limit_bytes=VMEM_LIMIT),
        name="outproj_mlp",
    )(*args)


def kernel(x, g_attn, w_in, w_out, rel_bias, lambda_qk, subln_g, g_ffn,
           w_gate_up, w_down, g_final):
    b, s, d = x.shape
    depth = w_in.shape[0]
    assert s % (DILATIONS[-1][1] * BAND) == 0 and s % DIFF_TILE == 0
    assert (b * s) % ROW_TILE == 0

    bias_dil = _bias_call(rel_bias, _dilated_buckets(), 0, N_HEADS_DIL)
    bias_dil = bias_dil.transpose(1, 0, 2, 3)
    bias_diff = _bias_call(rel_bias, _diff_buckets(s, DIFF_TILE), N_HEADS_DIL, N_HEADS_DIFF)

    col = jnp.arange(w_in.shape[-1]) // DIL_WIDTH
    qscale = jnp.where((col == 0) | (col == 3), HEAD_DIM ** -0.5, 1.0).astype(F32)

    x2 = x.reshape(b * s, d)
    for l in range(depth):
        lam_init = 0.8 - 0.6 * math.exp(-0.3 * l)
        w_in_l = (w_in[l] * qscale).astype(BF16)
        proj = _inproj_call(x2, g_attn[l][None], w_in_l).reshape(b, s, -1)
        oa = _dilated_attention(proj, bias_dil)
        ob = _diff_call(proj, bias_diff, lambda_qk[l][None], subln_g[l][None, None],
                        lam_init=lam_init)
        x2 = _mlp_call(x2, oa.reshape(b * s, DIL_WIDTH), ob.reshape(b * s, DIFF_WIDTH),
                       w_out[l].astype(BF16), g_ffn[l][None],
                       w_gate_up[l].astype(BF16), w_down[l].astype(BF16),
                       g_final[None] if l == depth - 1 else None)
    return x2.reshape(b, s, d)
```

```python
import functools
import math

import jax
import jax.numpy as jnp
from jax import lax
from jax.experimental import pallas as pl
from jax.experimental.pallas import tpu as pltpu

F32 = jnp.float32
BF16 = jnp.bfloat16

HEAD_DIM = 64
N_HEADS_DIL = 8
N_HEADS_DIFF = 4
DIL_WIDTH = N_HEADS_DIL * HEAD_DIM
DIFF_WIDTH = N_HEADS_DIFF * 2 * HEAD_DIM
MIX_WIDTH = DIL_WIDTH + DIFF_WIDTH
DILATIONS = ((128, 1), (512, 4), (2048, 16))
BAND = 128
N_BUCKETS = 32
MAX_DISTANCE = 2048
EPS = 1e-6
SUBLN_EPS = 1e-5
NEG = -1e30

LANES = 128
VMEM_LIMIT = 56 * 1024 * 1024
ROW_TILE = 512
DIFF_TILE = 256
FF_CHUNK = 768
STAT_L = N_HEADS_DIL


def _nt_dot(a, b):
    return lax.dot_general(a, b, (((1,), (1,)), ((), ())), preferred_element_type=F32)


def _dot(a, b):
    return jnp.dot(a, b, preferred_element_type=F32)


def _rms(x, g, eps):
    return (x * lax.rsqrt(jnp.mean(x * x, axis=-1, keepdims=True) + eps)) * g


def _rel_bucket(dist):
    max_exact = N_BUCKETS // 2
    n = jnp.maximum(dist, 0)
    nf = jnp.maximum(n, 1).astype(F32)
    large = max_exact + (jnp.log(nf / max_exact) / math.log(MAX_DISTANCE / max_exact)
                         * (N_BUCKETS - max_exact)).astype(jnp.int32)
    large = jnp.minimum(large, N_BUCKETS - 1)
    return jnp.where(n < max_exact, n, large)


def _bias_kernel(tab_ref, bkt_ref, o_ref, *, head0):
    h = pl.program_id(0) + head0
    bkt = bkt_ref[0]
    out = jnp.full(bkt.shape, NEG, F32)
    for b in range(N_BUCKETS):
        out = jnp.where(bkt == b, tab_ref[b, h], out)
    o_ref[0, 0] = out


def _bias_call(rel_bias, buckets, head0, n_heads):
    nt, r, c = buckets.shape
    return pl.pallas_call(
        functools.partial(_bias_kernel, head0=head0),
        grid=(n_heads, nt),
        in_specs=[pl.BlockSpec(memory_space=pltpu.SMEM),
                  pl.BlockSpec((1, r, c), lambda h, t: (t, 0, 0))],
        out_specs=pl.BlockSpec((1, 1, r, c), lambda h, t: (h, t, 0, 0)),
        out_shape=jax.ShapeDtypeStruct((n_heads, nt, r, c), F32),
        name="bias_tiles",
    )(rel_bias, buckets)


def _dilated_buckets():
    i = jnp.arange(BAND, dtype=jnp.int32)[:, None]
    j = jnp.arange(2 * BAND, dtype=jnp.int32)[None, :]
    m = i + BAND - j
    tiles = []
    for window, dil in DILATIONS:
        valid = (m >= 0) & (m <= window // dil)
        tiles.append(jnp.where(valid, _rel_bucket(m * dil), -1))
    return jnp.stack(tiles, axis=0)


def _diff_buckets(seq, tile):
    i = jnp.arange(tile, dtype=jnp.int32)[None, :, None]
    j = jnp.arange(tile, dtype=jnp.int32)[None, None, :]
    d = jnp.arange(seq // tile, dtype=jnp.int32)[:, None, None]
    dist = d * tile + i - j
    bkt = jnp.where(dist >= 0, _rel_bucket(dist), -1)
    return jnp.concatenate([bkt, jnp.full((1, tile, tile), -1, jnp.int32)], axis=0)


def _inproj_kernel(x_ref, g_ref, w_ref, o_ref):
    h = _rms(x_ref[...], g_ref[...], EPS).astype(BF16)
    o_ref[...] = _dot(h, w_ref[...]).astype(BF16)


def _inproj_call(x2, g, w):
    m, d = x2.shape
    n = w.shape[1]
    return pl.pallas_call(
        _inproj_kernel,
        grid=(m // ROW_TILE,),
        in_specs=[pl.BlockSpec((ROW_TILE, d), lambda i: (i, 0)),
                  pl.BlockSpec((1, d), lambda i: (0, 0)),
                  pl.BlockSpec((d, n), lambda i: (0, 0))],
        out_specs=pl.BlockSpec((ROW_TILE, n), lambda i: (i, 0)),
        out_shape=jax.ShapeDtypeStruct((m, n), BF16),
        compiler_params=pltpu.CompilerParams(
            dimension_semantics=("parallel",), vmem_limit_bytes=VMEM_LIMIT),
        name="inproj",
    )(x2, g, w)


def _dil_kernel(*refs, first, last):
    q_ref, kp_ref, kc_ref, vp_ref, vc_ref, bias_ref = refs[:6]
    refs = refs[6:]
    if not first:
        acc_in_ref, st_in_ref = refs[:2]
        refs = refs[2:]
    if last:
        (o_ref,) = refs
    else:
        acc_out_ref, st_out_ref = refs

    neg_prev = jnp.where(pl.program_id(2) == 0, NEG, 0.0).astype(F32)
    q = q_ref[0]
    kp, kc, vp, vc = kp_ref[0], kc_ref[0], vp_ref[0], vc_ref[0]
    if not first:
        st_in = st_in_ref[0]
        acc_in = acc_in_ref[0]
    lane = lax.broadcasted_iota(jnp.int32, (BAND, LANES), 1)
    st_out = jnp.zeros((BAND, LANES), F32)
    outs = []
    for h in range(N_HEADS_DIL):
        c = slice(h * HEAD_DIM, (h + 1) * HEAD_DIM)
        qh = q[:, c]
        bias = bias_ref[0, h]
        sp = _nt_dot(qh, kp[:, c]) + (bias[:, :BAND] + neg_prev)
        sc = _nt_dot(qh, kc[:, c]) + bias[:, BAND:]
        m_new = jnp.maximum(jnp.max(sp, axis=-1, keepdims=True),
                            jnp.max(sc, axis=-1, keepdims=True))
        if not first:
            m_old = st_in[:, h:h + 1]
            l_old = st_in[:, STAT_L + h:STAT_L + h + 1]
            m_new = jnp.maximum(m_new, m_old)
        pp = jnp.exp(sp - m_new)
        pc = jnp.exp(sc - m_new)
        l_new = jnp.sum(pp, axis=-1, keepdims=True) + jnp.sum(pc, axis=-1, keepdims=True)
        pv = _dot(pp.astype(BF16), vp[:, c]) + _dot(pc.astype(BF16), vc[:, c])
        if not first:
            alpha = jnp.exp(m_old - m_new)
            l_new = l_new + alpha * l_old
            pv = pv + alpha * acc_in[:, c]
        if last:
            outs.append(pv / l_new)
        else:
            outs.append(pv)
            st_out = jnp.where(lane == h, m_new, st_out)
            st_out = jnp.where(lane == STAT_L + h, l_new, st_out)
    out = jnp.concatenate(outs, axis=-1)
    if last:
        o_ref[0] = out.astype(BF16)
    else:
        acc_out_ref[0] = out
        st_out_ref[0] = st_out


def _dil_call(proj, bias, state, *, dil, dil_index, first, last):
    b, s, w = proj.shape
    nblk = w // DIL_WIDTH
    l = s // dil
    nb = l // BAND
    pv = proj.reshape(b, l, dil * w)

    def col(which):
        return lambda bi, r, n: (bi, n, r * nblk + which)

    def col_prev(which):
        return lambda bi, r, n: (bi, jnp.maximum(n - 1, 0), r * nblk + which)

    blk = (1, BAND, DIL_WIDTH)
    in_specs = [pl.BlockSpec(blk, col(0)),
                pl.BlockSpec(blk, col_prev(1)), pl.BlockSpec(blk, col(1)),
                pl.BlockSpec(blk, col_prev(2)), pl.BlockSpec(blk, col(2)),
                pl.BlockSpec((1, N_HEADS_DIL, BAND, 2 * BAND),
                             lambda bi, r, n: (dil_index, 0, 0, 0))]
    args = [pv, pv, pv, pv, pv, bias]
    state_specs = [pl.BlockSpec((1, BAND, DIL_WIDTH), lambda bi, r, n: (bi, n, r)),
                   pl.BlockSpec((1, BAND, LANES), lambda bi, r, n: (bi, n, r))]
    if not first:
        acc, st = state
        in_specs += state_specs
        args += [acc.reshape(b, l, dil * DIL_WIDTH), st.reshape(b, l, dil * LANES)]
    if last:
        out_specs = state_specs[0]
        out_shape = jax.ShapeDtypeStruct((b, l, dil * DIL_WIDTH), BF16)
    else:
        out_specs = state_specs
        out_shape = [jax.ShapeDtypeStruct((b, l, dil * DIL_WIDTH), F32),
                     jax.ShapeDtypeStruct((b, l, dil * LANES), F32)]
    res = pl.pallas_call(
        functools.partial(_dil_kernel, first=first, last=last),
        grid=(b, dil, nb),
        in_specs=in_specs,
        out_specs=out_specs,
        out_shape=out_shape,
        compiler_params=pltpu.CompilerParams(
            dimension_semantics=("parallel", "parallel", "arbitrary"),
            vmem_limit_bytes=VMEM_LIMIT),
        name=f"dilated_d{dil}",
    )(*args)
    if last:
        return res.reshape(b, s, DIL_WIDTH)
    return res[0].reshape(b, s, DIL_WIDTH), res[1].reshape(b, s, LANES)


def _dilated_attention(proj, bias):
    state = None
    n = len(DILATIONS)
    for idx, (_, dil) in enumerate(DILATIONS):
        state = _dil_call(proj, bias, state, dil=dil, dil_index=idx,
                          first=idx == 0, last=idx == n - 1)
    return state


def _diff_kernel(lq_ref, q_ref, k_ref, v_ref, bias_ref, g_ref, o_ref,
                 m_ref, acc_ref, sa_ref, sb_ref, *, lam_init):
    t = DIFF_TILE
    hw = 2 * HEAD_DIM
    qi = pl.program_id(2)
    n_delta = bias_ref.shape[1] - 1
    q = q_ref[0]
    lane = lax.broadcasted_iota(jnp.int32, q.shape, 1)
    zero = jnp.zeros_like(q)
    qq = jnp.concatenate([jnp.where(lane < HEAD_DIM, q, zero),
                          jnp.where(lane >= HEAD_DIM, q, zero)], axis=0)
    ones = jnp.ones((t, hw), BF16)
    m_ref[...] = jnp.full(m_ref.shape, NEG, F32)
    acc_ref[...] = jnp.zeros(acc_ref.shape, F32)

    def scores(kj):
        off = pl.multiple_of(jnp.minimum(kj, qi) * t, t)
        bias = bias_ref[0, jnp.where(kj > qi, n_delta, qi - kj)]
        s = _nt_dot(qq, k_ref[0, pl.ds(off, t), :])
        return (s.reshape(2, t, t) + bias[None]).reshape(2 * t, t)

    def accumulate(s, kj):
        off = pl.multiple_of(jnp.minimum(kj, qi) * t, t)
        v_ext = jnp.concatenate([v_ref[0, pl.ds(off, t), :], ones], axis=-1)
        m_prev = m_ref[...]
        m_next = jnp.maximum(m_prev, jnp.max(s, axis=-1, keepdims=True))
        p = jnp.exp(s - jnp.concatenate([m_next] * (t // LANES), axis=-1))
        alpha = jnp.exp(m_prev - m_next)
        acc_ref[...] = (jnp.concatenate([alpha, alpha], axis=-1) * acc_ref[...]
                        + _dot(p.astype(BF16), v_ext))
        m_ref[...] = m_next

    sa_ref[...] = scores(0)

    def body(jj, carry):
        k0 = 2 * jj
        sb_ref[...] = scores(k0 + 1)
        accumulate(sa_ref[...], k0)
        sa_ref[...] = scores(k0 + 2)
        accumulate(sb_ref[...], k0 + 1)
        return carry

    lax.fori_loop(0, (qi + 2) // 2, body, 0)

    lq = lq_ref[0]
    lam = (jnp.exp(jnp.sum(lq[0:1] * lq[1:2], axis=-1, keepdims=True))
           - jnp.exp(jnp.sum(lq[2:3] * lq[3:4], axis=-1, keepdims=True)) + lam_init)
    acc = acc_ref[...]
    o = acc[:, :hw] / acc[:, hw:]
    a = o[:t] - lam * o[t:]
    o_ref[0] = (_rms(a, g_ref[0], SUBLN_EPS) * (1.0 - lam_init)).astype(BF16)


def _diff_call(proj, bias, lq, g, *, lam_init):
    b, s, w = proj.shape
    t = DIFF_TILE
    hw = 2 * HEAD_DIM
    q0 = 3 * DIL_WIDTH // hw
    k0 = q0 + N_HEADS_DIFF
    v0 = k0 + N_HEADS_DIFF
    return pl.pallas_call(
        functools.partial(_diff_kernel, lam_init=lam_init),
        grid=(N_HEADS_DIFF, b, s // t),
        in_specs=[pl.BlockSpec((1, 4, HEAD_DIM), lambda h, bi, qi: (0, 0, 0)),
                  pl.BlockSpec((1, t, hw), lambda h, bi, qi: (bi, qi, q0 + h)),
                  pl.BlockSpec((1, s, hw), lambda h, bi, qi: (bi, 0, k0 + h)),
                  pl.BlockSpec((1, s, hw), lambda h, bi, qi: (bi, 0, v0 + h)),
                  pl.BlockSpec((1, s // t + 1, t, t), lambda h, bi, qi: (h, 0, 0, 0)),
                  pl.BlockSpec((1, 1, hw), lambda h, bi, qi: (0, 0, 0))],
        out_specs=pl.BlockSpec((1, t, hw), lambda h, bi, qi: (bi, qi, h)),
        out_shape=jax.ShapeDtypeStruct((b, s, DIFF_WIDTH), BF16),
        scratch_shapes=[pltpu.VMEM((2 * t, LANES), F32),
                        pltpu.VMEM((2 * t, 2 * hw), F32),
                        pltpu.VMEM((2 * t, t), F32),
                        pltpu.VMEM((2 * t, t), F32)],
        compiler_params=pltpu.CompilerParams(
            dimension_semantics=("parallel", "parallel", "arbitrary"),
            vmem_limit_bytes=VMEM_LIMIT),
        name="diff_attn",
    )(lq, proj, proj, proj, bias, g)


def _ff_chunks(d_ff):
    chunks, c0 = [], 0
    while c0 < d_ff:
        cw = min(FF_CHUNK, d_ff - c0)
        chunks.append((c0, cw))
        c0 += cw
    return chunks


def _mlp_kernel(*refs, final):
    x_ref, oa_ref, ob_ref, wo_ref, g_ref, wgu_ref, wd_ref = refs[:7]
    o_ref, act_ref = refs[-2:]
    d_ff = wd_ref.shape[0]
    x = (x_ref[...] + _dot(oa_ref[...], wo_ref[:DIL_WIDTH, :])
         + _dot(ob_ref[...], wo_ref[DIL_WIDTH:, :]))
    h = _rms(x, g_ref[...], EPS).astype(BF16)
    for c0, cw in _ff_chunks(d_ff):
        gate = _dot(h, wgu_ref[:, c0:c0 + cw])
        up = _dot(h, wgu_ref[:, d_ff + c0:d_ff + c0 + cw])
        act_ref[:, c0:c0 + cw] = ((gate * jax.nn.sigmoid(gate)) * up).astype(BF16)
    x = x + _dot(act_ref[...], wd_ref[...])
    if final:
        x = _rms(x, refs[7][...], EPS)
    o_ref[...] = x


def _mlp_call(x2, oa, ob, wo, g, wgu, wd, g_final):
    m, d = x2.shape
    d_ff = wd.shape[0]
    final = g_final is not None
    row = lambda i: (i, 0)
    fixed = lambda i: (0, 0)
    in_specs = [pl.BlockSpec((ROW_TILE, d), row),
                pl.BlockSpec((ROW_TILE, DIL_WIDTH), row),
                pl.BlockSpec((ROW_TILE, DIFF_WIDTH), row),
                pl.BlockSpec((MIX_WIDTH, d), fixed, pipeline_mode=pl.Buffered(1)),
                pl.BlockSpec((1, d), fixed),
                pl.BlockSpec((d, 2 * d_ff), fixed, pipeline_mode=pl.Buffered(1)),
                pl.BlockSpec((d_ff, d), fixed, pipeline_mode=pl.Buffered(1))]
    args = [x2, oa, ob, wo, g, wgu, wd]
    if final:
        in_specs.append(pl.BlockSpec((1, d), fixed))
        args.append(g_final)
    return pl.pallas_call(
        functools.partial(_mlp_kernel, final=final),
        grid=(m // ROW_TILE,),
        in_specs=in_specs,
        out_specs=pl.BlockSpec((ROW_TILE, d), row),
        out_shape=jax.ShapeDtypeStruct((m, d), F32),
        scratch_shapes=[pltpu.VMEM((ROW_TILE, d_ff), BF16)],
        compiler_params=pltpu.CompilerParams(
            dimension_semantics=("parallel",), vmem_limit_bytes=VMEM_LIMIT),
        name="outproj_mlp",
    )(*args)


def kernel(x, g_attn, w_in, w_out, rel_bias, lambda_qk, subln_g, g_ffn,
           w_gate_up, w_down, g_final):
    b, s, d = x.shape
    depth = w_in.shape[0]
    assert s % (DILATIONS[-1][1] * BAND) == 0 and s % DIFF_TILE == 0
    assert (b * s) % ROW_TILE == 0

    bias_dil = _bias_call(rel_bias, _dilated_buckets(), 0, N_HEADS_DIL)
    bias_dil = bias_dil.transpose(1, 0, 2, 3)
    bias_diff = _bias_call(rel_bias, _diff_buckets(s, DIFF_TILE), N_HEADS_DIL, N_HEADS_DIFF)

    col = jnp.arange(w_in.shape[-1]) // DIL_WIDTH
    qscale = jnp.where((col == 0) | (col == 3), HEAD_DIM ** -0.5, 1.0).astype(F32)

    x2 = x.reshape(b * s, d)
    for l in range(depth):
        lam_init = 0.8 - 0.6 * math.exp(-0.3 * l)
        w_in_l = (w_in[l] * qscale).astype(BF16)
        proj = _inproj_call(x2, g_attn[l][None], w_in_l).reshape(b, s, -1)
        oa = _dilated_attention(proj, bias_dil)
        ob = _diff_call(proj, bias_diff, lambda_qk[l][None], subln_g[l][None, None],
                        lam_init=lam_init)
        x2 = _mlp_call(x2, oa.reshape(b * s, DIL_WIDTH), ob.reshape(b * s, DIFF_WIDTH),
                       w_out[l].astype(BF16), g_ffn[l][None],
                       w_gate_up[l].astype(BF16), w_down[l].astype(BF16),
                       g_final[None] if l == depth - 1 else None)
    return x2.reshape(b, s, d)
```

```python
import functools
import math

import jax
import jax.numpy as jnp
from jax import lax
from jax.experimental import pallas as pl
from jax.experimental.pallas import tpu as pltpu

F32 = jnp.float32
BF16 = jnp.bfloat16

HEAD_DIM = 64
N_HEADS_DIL = 8
N_HEADS_DIFF = 4
DIL_WIDTH = N_HEADS_DIL * HEAD_DIM
DIFF_WIDTH = N_HEADS_DIFF * 2 * HEAD_DIM
MIX_WIDTH = DIL_WIDTH + DIFF_WIDTH
BAND = 128
N_BUCKETS = 32
MAX_DISTANCE = 2048
EPS = 1e-6
SUBLN_EPS = 1e-5
NEG = -1e30

LANES = 128
VMEM_LIMIT = 56 * 1024 * 1024
ROW_TILE = 512
TILE = 256
GROUP_ROWS = 16
DIFF_TILE = TILE
FF_CHUNK = 768


def _nt_dot(a, b):
    return lax.dot_general(a, b, (((1,), (1,)), ((), ())), preferred_element_type=F32)


def _dot(a, b):
    return jnp.dot(a, b, preferred_element_type=F32)


def _rms(x, g, eps):
    return (x * lax.rsqrt(jnp.mean(x * x, axis=-1, keepdims=True) + eps)) * g


def _rel_bucket(dist):
    max_exact = N_BUCKETS // 2
    n = jnp.maximum(dist, 0)
    nf = jnp.maximum(n, 1).astype(F32)
    large = max_exact + (jnp.log(nf / max_exact) / math.log(MAX_DISTANCE / max_exact)
                         * (N_BUCKETS - max_exact)).astype(jnp.int32)
    large = jnp.minimum(large, N_BUCKETS - 1)
    return jnp.where(n < max_exact, n, large)


def _bias_kernel(tab_ref, bkt_ref, o_ref, *, head0):
    h = pl.program_id(0) + head0
    bkt = bkt_ref[0]
    out = jnp.full(bkt.shape, NEG, F32)
    for b in range(N_BUCKETS):
        out = jnp.where(bkt == b, tab_ref[b, h], out)
    o_ref[0, 0] = out


def _bias_call(rel_bias, buckets, head0, n_heads):
    nt, r, c = buckets.shape
    return pl.pallas_call(
        functools.partial(_bias_kernel, head0=head0),
        grid=(n_heads, nt),
        in_specs=[pl.BlockSpec(memory_space=pltpu.SMEM),
                  pl.BlockSpec((1, r, c), lambda h, t: (t, 0, 0))],
        out_specs=pl.BlockSpec((1, 1, r, c), lambda h, t: (h, t, 0, 0)),
        out_shape=jax.ShapeDtypeStruct((n_heads, nt, r, c), F32),
        name="bias_tiles",
    )(rel_bias, buckets)


def _tile_token_offsets():
    pos = jnp.arange(TILE, dtype=jnp.int32)
    c, a, il = pos // 64, (pos // GROUP_ROWS) % 4, pos % GROUP_ROWS
    return GROUP_ROWS * il + 4 * a + c


def _band_buckets(mq, mk, dil, first):
    m = mq[:, None] - mk[None, :]
    valid = (m >= 0) & (m <= BAND)
    if first:
        valid = valid & (mk[None, :] >= 0)
    return jnp.where(valid, _rel_bucket(m * dil), -1)


def _dilated_buckets():
    half = TILE // 2
    r = jnp.arange(half, dtype=jnp.int32)
    g, il_lo = r // 8, r % 8
    w = GROUP_ROWS * il_lo + 4 * (g % 4) + g // 4
    mk = jnp.concatenate([w - half, w])
    tiles_a = jnp.stack([_band_buckets(w, mk, 1, f) for f in (False, True)])
    r = jnp.arange(64, dtype=jnp.int32)
    j = 4 * (r % GROUP_ROWS) + r // GROUP_ROWS
    mq = jnp.concatenate([j, j + 64])
    mk = jnp.concatenate([j - 128, j - 64, j, j + 64])
    tiles_b = jnp.stack([_band_buckets(mq, mk, 4, f) for f in (False, True)])
    i = jnp.arange(TILE, dtype=jnp.int32)
    tiles_c = _band_buckets(i, i, 16, False)[None]
    return tiles_a, tiles_b, tiles_c


def _diff_buckets(seq, tile):
    u = _tile_token_offsets()
    i = u[None, :, None]
    j = u[None, None, :]
    d = jnp.arange(seq // tile, dtype=jnp.int32)[:, None, None]
    dist = d * tile + i - j
    bkt = jnp.where(dist >= 0, _rel_bucket(dist), -1)
    return jnp.concatenate([bkt, jnp.full((1, tile, tile), -1, jnp.int32)], axis=0)


def _inproj_kernel(x_ref, g_ref, w_ref, o_ref):
    h = _rms(x_ref[...], g_ref[...], EPS).astype(BF16)
    o_ref[...] = _dot(h, w_ref[...]).astype(BF16)


def _inproj_call(x2, g, w):
    m, d = x2.shape
    n = w.shape[1]
    return pl.pallas_call(
        _inproj_kernel,
        grid=(m // ROW_TILE,),
        in_specs=[pl.BlockSpec((ROW_TILE, d), lambda i: (i, 0)),
                  pl.BlockSpec((1, d), lambda i: (0, 0)),
                  pl.BlockSpec((d, n), lambda i: (0, 0))],
        out_specs=pl.BlockSpec((ROW_TILE, n), lambda i: (i, 0)),
        out_shape=jax.ShapeDtypeStruct((m, n), BF16),
        compiler_params=pltpu.CompilerParams(
            dimension_semantics=("parallel",), vmem_limit_bytes=VMEM_LIMIT),
        name="inproj",
    )(x2, g, w)


def _head_masks(shape, dtype):
    lane = lax.broadcasted_iota(jnp.int32, shape, 1)
    return [jnp.where(sel, 1.0, 0.0).astype(dtype) for sel in (lane < HEAD_DIM, lane >= HEAD_DIM)]


def _v_ext(v):
    rows = [jnp.concatenate([v * mk, mk], axis=-1) for mk in _head_masks(v.shape, v.dtype)]
    return jnp.concatenate(rows, axis=0)


def _attend(q, k, v, bias, m_old, accl_old):
    r, kk = q.shape[0], k.shape[0]
    ps, m_new = [], []
    for h, mk in enumerate(_head_masks(q.shape, q.dtype)):
        s = _nt_dot(q * mk, k) + bias[h]
        mx = jnp.broadcast_to(jnp.max(s, axis=-1, keepdims=True), (r, LANES))
        mn = mx if m_old is None else jnp.maximum(m_old[h], mx)
        ps.append(jnp.exp(s - jnp.concatenate([mn] * (kk // LANES), axis=-1)).astype(BF16))
        m_new.append(mn)
    accl = _dot(jnp.concatenate(ps, axis=-1), _v_ext(v))
    if m_old is not None:
        first_head = lax.broadcasted_iota(jnp.int32, (r, LANES), 1) < HEAD_DIM
        alpha = jnp.where(first_head, jnp.exp(m_old[0] - m_new[0]), jnp.exp(m_old[1] - m_new[1]))
        accl = jnp.concatenate([alpha, alpha], axis=-1) * accl_old + accl
    return m_new, accl


def _dil_kernel(q_ref, k_ref, v_ref, ba_ref, bb_ref, bc_ref, o_ref, m0_ref, m1_ref, accl_ref):
    n_tiles = q_ref.shape[1] // TILE
    half = TILE // 2
    m_refs = (m0_ref, m1_ref)

    def halves(ref, n):
        x = ref[0, pl.ds(pl.multiple_of(n * TILE, TILE), TILE), :].astype(F32)
        x = x.reshape(TILE // GROUP_ROWS, 2, 8, LANES)
        return [x[:, hh].reshape(half, LANES).astype(BF16) for hh in (0, 1)]

    def whole(x0, x1):
        w = x0.shape[-1]
        parts = [x.reshape(TILE // GROUP_ROWS, 1, 8, w) for x in (x0, x1)]
        return jnp.concatenate(parts, axis=1).reshape(TILE, w)

    def pass_a(n, carry):
        qh, kh, vh = halves(q_ref, n), halves(k_ref, n), halves(v_ref, n)
        prev = jnp.maximum(n - 1, 0)
        k_prev, v_prev = halves(k_ref, prev)[1], halves(v_ref, prev)[1]
        first = (n == 0).astype(jnp.int32)
        res = []
        for hh in (0, 1):
            variant = first if hh == 0 else 0
            res.append(_attend(qh[hh],
                               jnp.concatenate([k_prev if hh == 0 else kh[0], kh[hh]], axis=0),
                               jnp.concatenate([v_prev if hh == 0 else vh[0], vh[hh]], axis=0),
                               [ba_ref[h, variant] for h in (0, 1)], None, None))
        rows = pl.ds(pl.multiple_of(n * TILE, TILE), TILE)
        accl_ref[rows, :] = whole(res[0][1], res[1][1])
        for h in (0, 1):
            m_refs[h][rows, :] = whole(res[0][0][h], res[1][0][h])
        return carry

    lax.fori_loop(0, n_tiles, pass_a, 0)

    def pass_b(nb, carry):
        t0 = 2 * nb
        first = (nb == 0).astype(jnp.int32)
        key_tiles = [jnp.maximum(t0 - 2, 0), jnp.maximum(t0 - 1, 0), t0, t0 + 1]
        for c in range(4):
            def chunks(ref, tiles, batch=None):
                parts = []
                for t in tiles:
                    rows = pl.ds(pl.multiple_of(t * TILE + c * 64, 64), 64)
                    parts.append(ref[rows, :] if batch is None else ref[batch, rows, :])
                return jnp.concatenate(parts, axis=0)
            m_new, accl = _attend(chunks(q_ref, [t0, t0 + 1], 0), chunks(k_ref, key_tiles, 0),
                                  chunks(v_ref, key_tiles, 0),
                                  [bb_ref[h, first] for h in (0, 1)],
                                  [chunks(m_refs[h], [t0, t0 + 1]) for h in (0, 1)],
                                  chunks(accl_ref, [t0, t0 + 1]))
            for i, t in enumerate((t0, t0 + 1)):
                rows = pl.ds(pl.multiple_of(t * TILE + c * 64, 64), 64)
                accl_ref[rows, :] = accl[64 * i:64 * (i + 1)]
                for h in (0, 1):
                    m_refs[h][rows, :] = m_new[h][64 * i:64 * (i + 1)]
        return carry

    lax.fori_loop(0, n_tiles // 2, pass_b, 0)

    def pass_c(a, carry):
        for c in range(4):
            def atoms(ref, batch=None):
                parts = []
                for t in range(n_tiles):
                    rows = pl.ds(pl.multiple_of(t * TILE + c * 64 + a * GROUP_ROWS, GROUP_ROWS),
                                 GROUP_ROWS)
                    parts.append(ref[rows, :] if batch is None else ref[batch, rows, :])
                return jnp.concatenate(parts, axis=0)
            _, accl = _attend(atoms(q_ref, 0), atoms(k_ref, 0), atoms(v_ref, 0),
                              [bc_ref[h, 0] for h in (0, 1)],
                              [atoms(m_refs[h]) for h in (0, 1)], atoms(accl_ref))
            out = (accl[:, :LANES] / accl[:, LANES:]).astype(BF16)
            for t in range(n_tiles):
                rows = pl.ds(pl.multiple_of(t * TILE + c * 64 + a * GROUP_ROWS, GROUP_ROWS),
                             GROUP_ROWS)
                o_ref[0, rows, :] = out[GROUP_ROWS * t:GROUP_ROWS * (t + 1)]
        return carry

    lax.fori_loop(0, 4, pass_c, 0)


def _dil_call(proj, bias_a, bias_b, bias_c):
    b, s, w = proj.shape
    assert s == TILE * GROUP_ROWS
    hw = 2 * HEAD_DIM
    n_pairs = N_HEADS_DIL // 2
    k0, v0 = DIL_WIDTH // hw, 2 * DIL_WIDTH // hw

    def bias_spec(bias):
        return pl.BlockSpec((2,) + bias.shape[1:], lambda hp, bi: (hp, 0, 0, 0))

    return pl.pallas_call(
        _dil_kernel,
        grid=(n_pairs, b),
        in_specs=[pl.BlockSpec((1, s, hw), lambda hp, bi: (bi, 0, hp)),
                  pl.BlockSpec((1, s, hw), lambda hp, bi: (bi, 0, k0 + hp)),
                  pl.BlockSpec((1, s, hw), lambda hp, bi: (bi, 0, v0 + hp)),
                  bias_spec(bias_a), bias_spec(bias_b), bias_spec(bias_c)],
        out_specs=pl.BlockSpec((1, s, hw), lambda hp, bi: (bi, 0, hp)),
        out_shape=jax.ShapeDtypeStruct((b, s, DIL_WIDTH), BF16),
        scratch_shapes=[pltpu.VMEM((s, LANES), F32),
                        pltpu.VMEM((s, LANES), F32),
                        pltpu.VMEM((s, 2 * LANES), F32)],
        compiler_params=pltpu.CompilerParams(
            dimension_semantics=("parallel", "parallel"), vmem_limit_bytes=VMEM_LIMIT),
        name="dilated_attn",
    )(proj, proj, proj, bias_a, bias_b, bias_c)


def _diff_kernel(lq_ref, q_ref, k_ref, v_ref, bias_ref, g_ref, o_ref,
                 m_ref, acc_ref, sa_ref, sb_ref, *, lam_init):
    t = DIFF_TILE
    hw = 2 * HEAD_DIM
    qi = pl.program_id(2)
    n_delta = bias_ref.shape[1] - 1
    q = q_ref[0]
    lane = lax.broadcasted_iota(jnp.int32, q.shape, 1)
    zero = jnp.zeros_like(q)
    qq = jnp.concatenate([jnp.where(lane < HEAD_DIM, q, zero),
                          jnp.where(lane >= HEAD_DIM, q, zero)], axis=0)
    ones = jnp.ones((t, hw), BF16)
    m_ref[...] = jnp.full(m_ref.shape, NEG, F32)
    acc_ref[...] = jnp.zeros(acc_ref.shape, F32)

    def scores(kj):
        off = pl.multiple_of(jnp.minimum(kj, qi) * t, t)
        bias = bias_ref[0, jnp.where(kj > qi, n_delta, qi - kj)]
        s = _nt_dot(qq, k_ref[0, pl.ds(off, t), :])
        return (s.reshape(2, t, t) + bias[None]).reshape(2 * t, t)

    def accumulate(s, kj):
        off = pl.multiple_of(jnp.minimum(kj, qi) * t, t)
        v_ext = jnp.concatenate([v_ref[0, pl.ds(off, t), :], ones], axis=-1)
        m_prev = m_ref[...]
        m_next = jnp.maximum(m_prev, jnp.max(s, axis=-1, keepdims=True))
        p = jnp.exp(s - jnp.concatenate([m_next] * (t // LANES), axis=-1))
        alpha = jnp.exp(m_prev - m_next)
        acc_ref[...] = (jnp.concatenate([alpha, alpha], axis=-1) * acc_ref[...]
                        + _dot(p.astype(BF16), v_ext))
        m_ref[...] = m_next

    sa_ref[...] = scores(0)

    def body(jj, carry):
        k0 = 2 * jj
        sb_ref[...] = scores(k0 + 1)
        accumulate(sa_ref[...], k0)
        sa_ref[...] = scores(k0 + 2)
        accumulate(sb_ref[...], k0 + 1)
        return carry

    lax.fori_loop(0, (qi + 2) // 2, body, 0)

    lq = lq_ref[0]
    lam = (jnp.exp(jnp.sum(lq[0:1] * lq[1:2], axis=-1, keepdims=True))
           - jnp.exp(jnp.sum(lq[2:3] * lq[3:4], axis=-1, keepdims=True)) + lam_init)
    acc = acc_ref[...]
    o = acc[:, :hw] / acc[:, hw:]
    a = o[:t] - lam * o[t:]
    o_ref[0] = (_rms(a, g_ref[0], SUBLN_EPS) * (1.0 - lam_init)).astype(BF16)


def _diff_call(proj, bias, lq, g, *, lam_init):
    b, s, w = proj.shape
    t = DIFF_TILE
    hw = 2 * HEAD_DIM
    q0 = 3 * DIL_WIDTH // hw
    k0 = q0 + N_HEADS_DIFF
    v0 = k0 + N_HEADS_DIFF
    return pl.pallas_call(
        functools.partial(_diff_kernel, lam_init=lam_init),
        grid=(N_HEADS_DIFF, b, s // t),
        in_specs=[pl.BlockSpec((1, 4, HEAD_DIM), lambda h, bi, qi: (0, 0, 0)),
                  pl.BlockSpec((1, t, hw), lambda h, bi, qi: (bi, qi, q0 + h)),
                  pl.BlockSpec((1, s, hw), lambda h, bi, qi: (bi, 0, k0 + h)),
                  pl.BlockSpec((1, s, hw), lambda h, bi, qi: (bi, 0, v0 + h)),
                  pl.BlockSpec((1, s // t + 1, t, t), lambda h, bi, qi: (h, 0, 0, 0)),
                  pl.BlockSpec((1, 1, hw), lambda h, bi, qi: (0, 0, 0))],
        out_specs=pl.BlockSpec((1, t, hw), lambda h, bi, qi: (bi, qi, h)),
        out_shape=jax.ShapeDtypeStruct((b, s, DIFF_WIDTH), BF16),
        scratch_shapes=[pltpu.VMEM((2 * t, LANES), F32),
                        pltpu.VMEM((2 * t, 2 * hw), F32),
                        pltpu.VMEM((2 * t, t), F32),
                        pltpu.VMEM((2 * t, t), F32)],
        compiler_params=pltpu.CompilerParams(
            dimension_semantics=("parallel", "parallel", "arbitrary"),
            vmem_limit_bytes=VMEM_LIMIT),
        name="diff_attn",
    )(lq, proj, proj, proj, bias, g)


def _ff_chunks(d_ff):
    chunks, c0 = [], 0
    while c0 < d_ff:
        cw = min(FF_CHUNK, d_ff - c0)
        chunks.append((c0, cw))
        c0 += cw
    return chunks


def _mlp_kernel(*refs, final):
    x_ref, oa_ref, ob_ref, wo_ref, g_ref, wgu_ref, wd_ref = refs[:7]
    o_ref, act_ref = refs[-2:]
    d_ff = wd_ref.shape[0]
    x = (x_ref[...] + _dot(oa_ref[...], wo_ref[:DIL_WIDTH, :])
         + _dot(ob_ref[...], wo_ref[DIL_WIDTH:, :]))
    h = _rms(x, g_ref[...], EPS).astype(BF16)
    for c0, cw in _ff_chunks(d_ff):
        gate = _dot(h, wgu_ref[:, c0:c0 + cw])
        up = _dot(h, wgu_ref[:, d_ff + c0:d_ff + c0 + cw])
        act_ref[:, c0:c0 + cw] = ((gate * jax.nn.sigmoid(gate)) * up).astype(BF16)
    x = x + _dot(act_ref[...], wd_ref[...])
    if final:
        x = _rms(x, refs[7][...], EPS)
    o_ref[...] = x


def _mlp_call(x2, oa, ob, wo, g, wgu, wd, g_final):
    m, d = x2.shape
    d_ff = wd.shape[0]
    final = g_final is not None
    row = lambda i: (i, 0)
    fixed = lambda i: (0, 0)
    in_specs = [pl.BlockSpec((ROW_TILE, d), row),
                pl.BlockSpec((ROW_TILE, DIL_WIDTH), row),
                pl.BlockSpec((ROW_TILE, DIFF_WIDTH), row),
                pl.BlockSpec((MIX_WIDTH, d), fixed, pipeline_mode=pl.Buffered(1)),
                pl.BlockSpec((1, d), fixed),
                pl.BlockSpec((d, 2 * d_ff), fixed, pipeline_mode=pl.Buffered(1)),
                pl.BlockSpec((d_ff, d), fixed, pipeline_mode=pl.Buffered(1))]
    args = [x2, oa, ob, wo, g, wgu, wd]
    if final:
        in_specs.append(pl.BlockSpec((1, d), fixed))
        args.append(g_final)
    return pl.pallas_call(
        functools.partial(_mlp_kernel, final=final),
        grid=(m // ROW_TILE,),
        in_specs=in_specs,
        out_specs=pl.BlockSpec((ROW_TILE, d), row),
        out_shape=jax.ShapeDtypeStruct((m, d), F32),
        scratch_shapes=[pltpu.VMEM((ROW_TILE, d_ff), BF16)],
        compiler_params=pltpu.CompilerParams(
            dimension_semantics=("parallel",), vmem_limit_bytes=VMEM_LIMIT),
        name="outproj_mlp",
    )(*args)


def kernel(x, g_attn, w_in, w_out, rel_bias, lambda_qk, subln_g, g_ffn,
           w_gate_up, w_down, g_final):
    b, s, d = x.shape
    depth = w_in.shape[0]
    assert (b * s) % ROW_TILE == 0
    nt = s // TILE

    bias_a, bias_b, bias_c = (_bias_call(rel_bias, bk, 0, N_HEADS_DIL) for bk in _dilated_buckets())
    bias_diff = _bias_call(rel_bias, _diff_buckets(s, DIFF_TILE), N_HEADS_DIL, N_HEADS_DIFF)

    col = jnp.arange(w_in.shape[-1]) // DIL_WIDTH
    qscale = jnp.where((col == 0) | (col == 3), HEAD_DIM ** -0.5, 1.0).astype(F32)

    x2 = x.reshape(b, nt, GROUP_ROWS, 4, 4, d).transpose(0, 1, 4, 3, 2, 5).reshape(b * s, d)
    for l in range(depth):
        lam_init = 0.8 - 0.6 * math.exp(-0.3 * l)
        w_in_l = (w_in[l] * qscale).astype(BF16)
        proj = _inproj_call(x2, g_attn[l][None], w_in_l).reshape(b, s, -1)
        oa = _dil_call(proj, bias_a, bias_b, bias_c)
        ob = _diff_call(proj, bias_diff, lambda_qk[l][None], subln_g[l][None, None],
                        lam_init=lam_init)
        x2 = _mlp_call(x2, oa.reshape(b * s, DIL_WIDTH), ob.reshape(b * s, DIFF_WIDTH),
                       w_out[l].astype(BF16), g_ffn[l][None],
                       w_gate_up[l].astype(BF16), w_down[l].astype(BF16),
                       g_final[None] if l == depth - 1 else None)
    return x2.reshape(b, nt, 4, 4, GROUP_ROWS, d).transpose(0, 1, 4, 3, 2, 5).reshape(b, s, d)
```

```python
import functools
import math

import jax
import jax.numpy as jnp
from jax import lax
from jax.experimental import pallas as pl
from jax.experimental.pallas import tpu as pltpu

F32 = jnp.float32
BF16 = jnp.bfloat16

HEAD_DIM = 64
N_HEADS_DIL = 8
N_HEADS_DIFF = 4
DIL_WIDTH = N_HEADS_DIL * HEAD_DIM
DIFF_WIDTH = N_HEADS_DIFF * 2 * HEAD_DIM
MIX_WIDTH = DIL_WIDTH + DIFF_WIDTH
BAND = 128
N_BUCKETS = 32
MAX_DISTANCE = 2048
EPS = 1e-6
SUBLN_EPS = 1e-5
NEG = -1e30
LOG2E = math.log2(math.e)

LANES = 128
VMEM_LIMIT = 56 * 1024 * 1024
ROW_TILE = 512
TILE = 256
GROUP_ROWS = 16
DIFF_TILE = TILE
FF_CHUNK = 768
ONES_ROWS = 16
DIFF_STREAMS = 2


def _nt_dot(a, b):
    return lax.dot_general(a, b, (((1,), (1,)), ((), ())), preferred_element_type=F32)


def _dot(a, b):
    return jnp.dot(a, b, preferred_element_type=F32)


def _rms(x, g, eps):
    return (x * lax.rsqrt(jnp.mean(x * x, axis=-1, keepdims=True) + eps)) * g


def _rel_bucket(dist):
    max_exact = N_BUCKETS // 2
    n = jnp.maximum(dist, 0)
    nf = jnp.maximum(n, 1).astype(F32)
    large = max_exact + (jnp.log(nf / max_exact) / math.log(MAX_DISTANCE / max_exact)
                         * (N_BUCKETS - max_exact)).astype(jnp.int32)
    large = jnp.minimum(large, N_BUCKETS - 1)
    return jnp.where(n < max_exact, n, large)


def _bias_kernel(tab_ref, bkt_ref, o_ref, *, head0):
    h = pl.program_id(0) + head0
    bkt = bkt_ref[0]
    out = jnp.full(bkt.shape, NEG, F32)
    for b in range(N_BUCKETS):
        out = jnp.where(bkt == b, tab_ref[b, h] * LOG2E, out)
    o_ref[0, 0] = out


def _bias_call(rel_bias, buckets, head0, n_heads):
    nt, r, c = buckets.shape
    return pl.pallas_call(
        functools.partial(_bias_kernel, head0=head0),
        grid=(n_heads, nt),
        in_specs=[pl.BlockSpec(memory_space=pltpu.SMEM),
                  pl.BlockSpec((1, r, c), lambda h, t: (t, 0, 0))],
        out_specs=pl.BlockSpec((1, 1, r, c), lambda h, t: (h, t, 0, 0)),
        out_shape=jax.ShapeDtypeStruct((n_heads, nt, r, c), F32),
        name="bias_tiles",
    )(rel_bias, buckets)


def _tile_token_offsets():
    pos = jnp.arange(TILE, dtype=jnp.int32)
    c, a, il = pos // 64, (pos // GROUP_ROWS) % 4, pos % GROUP_ROWS
    return GROUP_ROWS * il + 4 * a + c


def _band_buckets(mq, mk, dil, first):
    m = mq[:, None] - mk[None, :]
    valid = (m >= 0) & (m <= BAND)
    if first:
        valid = valid & (mk[None, :] >= 0)
    return jnp.where(valid, _rel_bucket(m * dil), -1)


def _dilated_buckets():
    half = TILE // 2
    r = jnp.arange(half, dtype=jnp.int32)
    g, il_lo = r // 8, r % 8
    w = GROUP_ROWS * il_lo + 4 * (g % 4) + g // 4
    mk = jnp.concatenate([w - half, w])
    tiles_a = jnp.stack([_band_buckets(w, mk, 1, f) for f in (False, True)])
    r = jnp.arange(64, dtype=jnp.int32)
    j = 4 * (r % GROUP_ROWS) + r // GROUP_ROWS
    mq = jnp.concatenate([j, j + 64])
    mk = jnp.concatenate([j - 128, j - 64, j, j + 64])
    tiles_b = jnp.stack([_band_buckets(mq, mk, 4, f) for f in (False, True)])
    i = jnp.arange(TILE, dtype=jnp.int32)
    tiles_c = _band_buckets(i, i, 16, False)[None]
    return tiles_a, tiles_b, tiles_c


def _diff_buckets(seq, tile):
    u = _tile_token_offsets()
    i = u[None, None, :]
    j = u[None, :, None]
    d = jnp.arange(seq // tile, dtype=jnp.int32)[:, None, None]
    dist = d * tile + i - j
    bkt = jnp.where(dist >= 0, _rel_bucket(dist), -1)
    return jnp.concatenate([bkt, jnp.full((1, tile, tile), -1, jnp.int32)], axis=0)


def _inproj_kernel(x_ref, g_ref, w_ref, wvt_ref, cs_ref, o_ref, vt_ref):
    h = _rms(x_ref[...], g_ref[...], EPS).astype(BF16)
    o_ref[...] = (_dot(h, w_ref[...]) * cs_ref[...]).astype(BF16)
    vt = _nt_dot(wvt_ref[...], h).astype(BF16)
    for j in range(vt_ref.shape[0]):
        vt_ref[j] = vt[:, j * DIFF_TILE:(j + 1) * DIFF_TILE]


def _inproj_call(x2, g, w, wvt, colscale):
    m, d = x2.shape
    n, f = w.shape[1], wvt.shape[0]
    per_step = ROW_TILE // DIFF_TILE
    return pl.pallas_call(
        _inproj_kernel,
        grid=(m // ROW_TILE,),
        in_specs=[pl.BlockSpec((ROW_TILE, d), lambda i: (i, 0)),
                  pl.BlockSpec((1, d), lambda i: (0, 0)),
                  pl.BlockSpec((d, n), lambda i: (0, 0)),
                  pl.BlockSpec((f, d), lambda i: (0, 0)),
                  pl.BlockSpec((1, n), lambda i: (0, 0))],
        out_specs=[pl.BlockSpec((ROW_TILE, n), lambda i: (i, 0)),
                   pl.BlockSpec((per_step, f, DIFF_TILE), lambda i: (i, 0, 0))],
        out_shape=[jax.ShapeDtypeStruct((m, n), BF16),
                   jax.ShapeDtypeStruct((m // DIFF_TILE, f, DIFF_TILE), BF16)],
        compiler_params=pltpu.CompilerParams(
            dimension_semantics=("parallel",), vmem_limit_bytes=VMEM_LIMIT),
        name="inproj",
    )(x2, g, w, wvt, colscale)


def _head_masks(shape, dtype):
    lane = lax.broadcasted_iota(jnp.int32, shape, 1)
    return [jnp.where(sel, 1.0, 0.0).astype(dtype) for sel in (lane < HEAD_DIM, lane >= HEAD_DIM)]


def _v_ext(v):
    rows = [jnp.concatenate([v * mk, mk], axis=-1) for mk in _head_masks(v.shape, v.dtype)]
    return jnp.concatenate(rows, axis=0)


def _attend(q, k, v, bias, m_old, accl_old):
    r, kk = q.shape[0], k.shape[0]
    ps, m_new = [], []
    for h, mk in enumerate(_head_masks(q.shape, q.dtype)):
        s = _nt_dot(q * mk, k) + bias[h]
        mx = jnp.broadcast_to(jnp.max(s, axis=-1, keepdims=True), (r, LANES))
        mn = mx if m_old is None else jnp.maximum(m_old[h], mx)
        ps.append(jnp.exp2(s - jnp.concatenate([mn] * (kk // LANES), axis=-1)).astype(BF16))
        m_new.append(mn)
    accl = _dot(jnp.concatenate(ps, axis=-1), _v_ext(v))
    if m_old is not None:
        first_head = lax.broadcasted_iota(jnp.int32, (r, LANES), 1) < HEAD_DIM
        alpha = jnp.where(first_head, jnp.exp2(m_old[0] - m_new[0]), jnp.exp2(m_old[1] - m_new[1]))
        accl = jnp.concatenate([alpha, alpha], axis=-1) * accl_old + accl
    return m_new, accl


def _dil_kernel(q_ref, k_ref, v_ref, ba_ref, bb_ref, bc_ref, o_ref, m0_ref, m1_ref, accl_ref):
    n_tiles = q_ref.shape[1] // TILE
    half = TILE // 2
    m_refs = (m0_ref, m1_ref)

    def halves(ref, n):
        x = ref[0, pl.ds(pl.multiple_of(n * TILE, TILE), TILE), :].astype(F32)
        x = x.reshape(TILE // GROUP_ROWS, 2, 8, LANES)
        return [x[:, hh].reshape(half, LANES).astype(BF16) for hh in (0, 1)]

    def whole(x0, x1):
        w = x0.shape[-1]
        parts = [x.reshape(TILE // GROUP_ROWS, 1, 8, w) for x in (x0, x1)]
        return jnp.concatenate(parts, axis=1).reshape(TILE, w)

    def pass_a(n, carry):
        qh, kh, vh = halves(q_ref, n), halves(k_ref, n), halves(v_ref, n)
        prev = jnp.maximum(n - 1, 0)
        k_prev, v_prev = halves(k_ref, prev)[1], halves(v_ref, prev)[1]
        first = jnp.where(n == 0, 1, 0)
        res = []
        for hh in (0, 1):
            variant = first if hh == 0 else 0
            res.append(_attend(qh[hh],
                               jnp.concatenate([k_prev if hh == 0 else kh[0], kh[hh]], axis=0),
                               jnp.concatenate([v_prev if hh == 0 else vh[0], vh[hh]], axis=0),
                               [ba_ref[h, variant] for h in (0, 1)], None, None))
        rows = pl.ds(pl.multiple_of(n * TILE, TILE), TILE)
        accl_ref[rows, :] = whole(res[0][1], res[1][1])
        for h in (0, 1):
            m_refs[h][rows, :] = whole(res[0][0][h], res[1][0][h])
        return carry

    lax.fori_loop(0, n_tiles, pass_a, 0)

    def pass_b(nb, carry):
        t0 = 2 * nb
        first = jnp.where(nb == 0, 1, 0)
        key_tiles = [jnp.maximum(t0 - 2, 0), jnp.maximum(t0 - 1, 0), t0, t0 + 1]
        for c in range(4):
            def chunks(ref, tiles, batch=None):
                parts = []
                for t in tiles:
                    rows = pl.ds(pl.multiple_of(t * TILE + c * 64, 64), 64)
                    parts.append(ref[rows, :] if batch is None else ref[batch, rows, :])
                return jnp.concatenate(parts, axis=0)
            m_new, accl = _attend(chunks(q_ref, [t0, t0 + 1], 0), chunks(k_ref, key_tiles, 0),
                                  chunks(v_ref, key_tiles, 0),
                                  [bb_ref[h, first] for h in (0, 1)],
                                  [chunks(m_refs[h], [t0, t0 + 1]) for h in (0, 1)],
                                  chunks(accl_ref, [t0, t0 + 1]))
            for i, t in enumerate((t0, t0 + 1)):
                rows = pl.ds(pl.multiple_of(t * TILE + c * 64, 64), 64)
                accl_ref[rows, :] = accl[64 * i:64 * (i + 1)]
                for h in (0, 1):
                    m_refs[h][rows, :] = m_new[h][64 * i:64 * (i + 1)]
        return carry

    lax.fori_loop(0, n_tiles // 2, pass_b, 0)

    def pass_c(a, carry):
        for c in range(4):
            def atoms(ref, batch=None):
                parts = []
                for t in range(n_tiles):
                    rows = pl.ds(pl.multiple_of(t * TILE + c * 64 + a * GROUP_ROWS, GROUP_ROWS),
                                 GROUP_ROWS)
                    parts.append(ref[rows, :] if batch is None else ref[batch, rows, :])
                return jnp.concatenate(parts, axis=0)
            _, accl = _attend(atoms(q_ref, 0), atoms(k_ref, 0), atoms(v_ref, 0),
                              [bc_ref[h, 0] for h in (0, 1)],
                              [atoms(m_refs[h]) for h in (0, 1)], atoms(accl_ref))
            out = (accl[:, :LANES] / accl[:, LANES:]).astype(BF16)
            for t in range(n_tiles):
                rows = pl.ds(pl.multiple_of(t * TILE + c * 64 + a * GROUP_ROWS, GROUP_ROWS),
                             GROUP_ROWS)
                o_ref[0, rows, :] = out[GROUP_ROWS * t:GROUP_ROWS * (t + 1)]
        return carry

    lax.fori_loop(0, 4, pass_c, 0)


def _dil_call(proj, bias_a, bias_b, bias_c):
    b, s, w = proj.shape
    assert s == TILE * GROUP_ROWS
    hw = 2 * HEAD_DIM
    n_pairs = N_HEADS_DIL // 2
    k0, v0 = DIL_WIDTH // hw, 2 * DIL_WIDTH // hw

    def bias_spec(bias):
        return pl.BlockSpec((2,) + bias.shape[1:], lambda hp, bi: (hp, 0, 0, 0))

    return pl.pallas_call(
        _dil_kernel,
        grid=(n_pairs, b),
        in_specs=[pl.BlockSpec((1, s, hw), lambda hp, bi: (bi, 0, hp)),
                  pl.BlockSpec((1, s, hw), lambda hp, bi: (bi, 0, k0 + hp)),
                  pl.BlockSpec((1, s, hw), lambda hp, bi: (bi, 0, v0 + hp)),
                  bias_spec(bias_a), bias_spec(bias_b), bias_spec(bias_c)],
        out_specs=pl.BlockSpec((1, s, hw), lambda hp, bi: (bi, 0, hp)),
        out_shape=jax.ShapeDtypeStruct((b, s, DIL_WIDTH), BF16),
        scratch_shapes=[pltpu.VMEM((s, LANES), F32),
                        pltpu.VMEM((s, LANES), F32),
                        pltpu.VMEM((s, 2 * LANES), F32)],
        compiler_params=pltpu.CompilerParams(
            dimension_semantics=("parallel", "parallel"), vmem_limit_bytes=VMEM_LIMIT),
        name="dilated_attn",
    )(proj, proj, proj, bias_a, bias_b, bias_c)


def _diff_kernel(lq_ref, q_ref, k_ref, vt_ref, bias_ref, g_ref, o_ref,
                 acc_ref, sa_ref, sb_ref, *, lam_init):
    t = DIFF_TILE
    hw = 2 * HEAD_DIM
    n_delta = bias_ref.shape[1] - 1
    ones = jnp.ones((ONES_ROWS, t), BF16)
    lq = lq_ref[0]
    lam = (jnp.exp(jnp.sum(lq[0:1] * lq[1:2], axis=-1, keepdims=True))
           - jnp.exp(jnp.sum(lq[2:3] * lq[3:4], axis=-1, keepdims=True)) + lam_init)

    n_tiles = q_ref.shape[1] // t
    streams = range(q_ref.shape[0])

    def q_block(qi, carry):
        rows = pl.ds(pl.multiple_of(qi * t, t), t)
        qq = []
        for e in streams:
            q = q_ref[e, rows, :]
            lane = lax.broadcasted_iota(jnp.int32, q.shape, 1)
            zero = jnp.zeros_like(q)
            qq.append(jnp.concatenate([jnp.where(lane < HEAD_DIM, q, zero),
                                       jnp.where(lane >= HEAD_DIM, q, zero)], axis=0))
        acc_ref[...] = jnp.zeros(acc_ref.shape, F32)

        def scores(e, kj, bias):
            off = pl.multiple_of(jnp.minimum(kj, qi) * t, t)
            return _nt_dot(k_ref[e, pl.ds(off, t), :], qq[e]) + bias

        def bias_of(kj):
            bias = bias_ref[0, jnp.where(kj > qi, n_delta, qi - kj)]
            return jnp.concatenate([bias, bias], axis=-1)

        def accumulate(e, s, kj, m_prev):
            vt_ext = jnp.concatenate([vt_ref[e * n_tiles + jnp.minimum(kj, qi)], ones], axis=0)
            m_next = jnp.maximum(m_prev, jnp.max(s, axis=0, keepdims=True))
            p = jnp.exp2(s - m_next).astype(BF16)
            acc_ref[e] = jnp.exp2(m_prev - m_next) * acc_ref[e] + _dot(vt_ext, p)
            return m_next

        bias0 = bias_of(0)
        for e in streams:
            sa_ref[e] = scores(e, 0, bias0)

        def body(jj, ms):
            k0 = 2 * jj
            bias = bias_of(k0 + 1)
            for e in streams:
                sb_ref[e] = scores(e, k0 + 1, bias)
            ms = [accumulate(e, sa_ref[e], k0, ms[e]) for e in streams]
            bias = bias_of(k0 + 2)
            for e in streams:
                sa_ref[e] = scores(e, k0 + 2, bias)
            return [accumulate(e, sb_ref[e], k0 + 1, ms[e]) for e in streams]

        lax.fori_loop(0, (qi + 2) // 2, body, [jnp.full((1, 2 * t), NEG, F32) for _ in streams])

        for e in streams:
            acc = acc_ref[e]
            o = acc[:hw] / acc[hw:hw + 1]
            a = (o[:, :t] - lam * o[:, t:]).T
            o_ref[e, rows, :] = (_rms(a, g_ref[0], SUBLN_EPS) * (1.0 - lam_init)).astype(BF16)
        return carry

    lax.fori_loop(0, n_tiles, q_block, 0)


def _diff_call(proj, vt, bias, lq, g, *, lam_init):
    b, s, w = proj.shape
    t = DIFF_TILE
    hw = 2 * HEAD_DIM
    q0 = 3 * DIL_WIDTH // hw
    k0 = q0 + N_HEADS_DIFF
    ns = DIFF_STREAMS
    assert b % ns == 0
    return pl.pallas_call(
        functools.partial(_diff_kernel, lam_init=lam_init),
        grid=(N_HEADS_DIFF, b // ns),
        in_specs=[pl.BlockSpec((1, 4, HEAD_DIM), lambda h, bi: (0, 0, 0)),
                  pl.BlockSpec((ns, s, hw), lambda h, bi: (bi, 0, q0 + h)),
                  pl.BlockSpec((ns, s, hw), lambda h, bi: (bi, 0, k0 + h)),
                  pl.BlockSpec((ns * (s // t), hw, t), lambda h, bi: (bi, h, 0)),
                  pl.BlockSpec((1, s // t + 1, t, t), lambda h, bi: (h, 0, 0, 0)),
                  pl.BlockSpec((1, 1, hw), lambda h, bi: (0, 0, 0))],
        out_specs=pl.BlockSpec((ns, s, hw), lambda h, bi: (bi, 0, h)),
        out_shape=jax.ShapeDtypeStruct((b, s, DIFF_WIDTH), BF16),
        scratch_shapes=[pltpu.VMEM((ns, hw + ONES_ROWS, 2 * t), F32),
                        pltpu.VMEM((ns, t, 2 * t), F32),
                        pltpu.VMEM((ns, t, 2 * t), F32)],
        compiler_params=pltpu.CompilerParams(
            dimension_semantics=("parallel", "parallel"), vmem_limit_bytes=VMEM_LIMIT),
        name="diff_attn",
    )(lq, proj, proj, vt, bias, g)


def _ff_chunks(d_ff):
    chunks, c0 = [], 0
    while c0 < d_ff:
        cw = min(FF_CHUNK, d_ff - c0)
        chunks.append((c0, cw))
        c0 += cw
    return chunks


def _mlp_kernel(*refs, final):
    x_ref, oa_ref, ob_ref, wo_ref, g_ref, wgu_ref, wd_ref = refs[:7]
    o_ref, act_ref = refs[-2:]
    d_ff = wd_ref.shape[0]
    x = (x_ref[...] + _dot(oa_ref[...], wo_ref[:DIL_WIDTH, :])
         + _dot(ob_ref[...], wo_ref[DIL_WIDTH:, :]))
    h = _rms(x, g_ref[...], EPS).astype(BF16)
    for c0, cw in _ff_chunks(d_ff):
        gate = _dot(h, wgu_ref[:, c0:c0 + cw])
        up = _dot(h, wgu_ref[:, d_ff + c0:d_ff + c0 + cw])
        act_ref[:, c0:c0 + cw] = ((gate * jax.nn.sigmoid(gate)) * up).astype(BF16)
    x = x + _dot(act_ref[...], wd_ref[...])
    if final:
        x = _rms(x, refs[7][...], EPS)
    o_ref[...] = x


def _mlp_call(x2, oa, ob, wo, g, wgu, wd, g_final):
    m, d = x2.shape
    d_ff = wd.shape[0]
    final = g_final is not None
    row = lambda i: (i, 0)
    fixed = lambda i: (0, 0)
    in_specs = [pl.BlockSpec((ROW_TILE, d), row),
                pl.BlockSpec((ROW_TILE, DIL_WIDTH), row),
                pl.BlockSpec((ROW_TILE, DIFF_WIDTH), row),
                pl.BlockSpec((MIX_WIDTH, d), fixed, pipeline_mode=pl.Buffered(1)),
                pl.BlockSpec((1, d), fixed),
                pl.BlockSpec((d, 2 * d_ff), fixed, pipeline_mode=pl.Buffered(1)),
                pl.BlockSpec((d_ff, d), fixed, pipeline_mode=pl.Buffered(1))]
    args = [x2, oa, ob, wo, g, wgu, wd]
    if final:
        in_specs.append(pl.BlockSpec((1, d), fixed))
        args.append(g_final)
    return pl.pallas_call(
        functools.partial(_mlp_kernel, final=final),
        grid=(m // ROW_TILE,),
        in_specs=in_specs,
        out_specs=pl.BlockSpec((ROW_TILE, d), row),
        out_shape=jax.ShapeDtypeStruct((m, d), F32),
        scratch_shapes=[pltpu.VMEM((ROW_TILE, d_ff), BF16)],
        compiler_params=pltpu.CompilerParams(
            dimension_semantics=("parallel",), vmem_limit_bytes=VMEM_LIMIT),
        name="outproj_mlp",
    )(*args)


def kernel(x, g_attn, w_in, w_out, rel_bias, lambda_qk, subln_g, g_ffn,
           w_gate_up, w_down, g_final):
    b, s, d = x.shape
    depth = w_in.shape[0]
    assert (b * s) % ROW_TILE == 0
    nt = s // TILE

    bias_a, bias_b, bias_c = (_bias_call(rel_bias, bk, 0, N_HEADS_DIL) for bk in _dilated_buckets())
    bias_diff = _bias_call(rel_bias, _diff_buckets(s, DIFF_TILE), N_HEADS_DIL, N_HEADS_DIFF)

    col = jnp.arange(w_in.shape[-1] - DIFF_WIDTH) // DIL_WIDTH
    qscale = jnp.where((col == 0) | (col == 3), LOG2E * HEAD_DIM ** -0.5, 1.0).astype(F32)[None]

    x2 = x.reshape(b, nt, GROUP_ROWS, 4, 4, d).transpose(0, 1, 4, 3, 2, 5).reshape(b * s, d)
    for l in range(depth):
        lam_init = 0.8 - 0.6 * math.exp(-0.3 * l)
        w_in_l = w_in[l].astype(BF16)
        n_tok = w_in_l.shape[1] - DIFF_WIDTH
        proj, vt = _inproj_call(x2, g_attn[l][None], w_in_l[:, :n_tok], w_in_l[:, n_tok:].T, qscale)
        proj = proj.reshape(b, s, -1)
        oa = _dil_call(proj, bias_a, bias_b, bias_c)
        ob = _diff_call(proj, vt, bias_diff, lambda_qk[l][None], subln_g[l][None, None],
                        lam_init=lam_init)
        x2 = _mlp_call(x2, oa.reshape(b * s, DIL_WIDTH), ob.reshape(b * s, DIFF_WIDTH),
                       w_out[l].astype(BF16), g_ffn[l][None],
                       w_gate_up[l].astype(BF16), w_down[l].astype(BF16),
                       g_final[None] if l == depth - 1 else None)
    return x2.reshape(b, nt, 4, 4, GROUP_ROWS, d).transpose(0, 1, 4, 3, 2, 5).reshape(b, s, d)
```

```python
import functools
import math

import jax
import jax.numpy as jnp
from jax import lax
from jax.experimental import pallas as pl
from jax.experimental.pallas import tpu as pltpu

F32 = jnp.float32
BF16 = jnp.bfloat16

HEAD_DIM = 64
N_HEADS_DIL = 8
N_HEADS_DIFF = 4
DIL_WIDTH = N_HEADS_DIL * HEAD_DIM
DIFF_WIDTH = N_HEADS_DIFF * 2 * HEAD_DIM
MIX_WIDTH = DIL_WIDTH + DIFF_WIDTH
BAND = 128
N_BUCKETS = 32
MAX_DISTANCE = 2048
EPS = 1e-6
SUBLN_EPS = 1e-5
NEG = -1e30
LOG2E = math.log2(math.e)

LANES = 128
VMEM_LIMIT = 56 * 1024 * 1024
ROW_TILE = 512
TILE = 256
GROUP_ROWS = 16
DIFF_TILE = TILE
FF_CHUNK = 768
ONES_ROWS = 16
DIFF_STREAMS = 2


def _nt_dot(a, b):
    return lax.dot_general(a, b, (((1,), (1,)), ((), ())), preferred_element_type=F32)


def _dot(a, b):
    return jnp.dot(a, b, preferred_element_type=F32)


def _rms(x, g, eps):
    return (x * lax.rsqrt(jnp.mean(x * x, axis=-1, keepdims=True) + eps)) * g


def _rel_bucket(dist):
    max_exact = N_BUCKETS // 2
    n = jnp.maximum(dist, 0)
    nf = jnp.maximum(n, 1).astype(F32)
    large = max_exact + (jnp.log(nf / max_exact) / math.log(MAX_DISTANCE / max_exact)
                         * (N_BUCKETS - max_exact)).astype(jnp.int32)
    large = jnp.minimum(large, N_BUCKETS - 1)
    return jnp.where(n < max_exact, n, large)


def _bias_kernel(tab_ref, bkt_ref, o_ref, *, head0):
    h = pl.program_id(0) + head0
    bkt = bkt_ref[0]
    out = jnp.full(bkt.shape, NEG, F32)
    for b in range(N_BUCKETS):
        out = jnp.where(bkt == b, tab_ref[b, h] * LOG2E, out)
    o_ref[0, 0] = out


def _bias_call(rel_bias, buckets, head0, n_heads):
    nt, r, c = buckets.shape
    return pl.pallas_call(
        functools.partial(_bias_kernel, head0=head0),
        grid=(n_heads, nt),
        in_specs=[pl.BlockSpec(memory_space=pltpu.SMEM),
                  pl.BlockSpec((1, r, c), lambda h, t: (t, 0, 0))],
        out_specs=pl.BlockSpec((1, 1, r, c), lambda h, t: (h, t, 0, 0)),
        out_shape=jax.ShapeDtypeStruct((n_heads, nt, r, c), F32),
        name="bias_tiles",
    )(rel_bias, buckets)


def _tile_token_offsets():
    pos = jnp.arange(TILE, dtype=jnp.int32)
    c, a, il = pos // 64, (pos // GROUP_ROWS) % 4, pos % GROUP_ROWS
    return GROUP_ROWS * il + 4 * a + c


def _band_buckets(mq, mk, dil, first):
    m = mq[:, None] - mk[None, :]
    valid = (m >= 0) & (m <= BAND)
    if first:
        valid = valid & (mk[None, :] >= 0)
    return jnp.where(valid, _rel_bucket(m * dil), -1)


def _dilated_buckets():
    half = TILE // 2
    r = jnp.arange(half, dtype=jnp.int32)
    g, il_lo = r // 8, r % 8
    w = GROUP_ROWS * il_lo + 4 * (g % 4) + g // 4
    mk = jnp.concatenate([w - half, w])
    tiles_a = jnp.stack([_band_buckets(w, mk, 1, f) for f in (False, True)])
    r = jnp.arange(64, dtype=jnp.int32)
    j = 4 * (r % GROUP_ROWS) + r // GROUP_ROWS
    mq = jnp.concatenate([j, j + 64])
    mk = jnp.concatenate([j - 128, j - 64, j, j + 64])
    tiles_b = jnp.stack([_band_buckets(mq, mk, 4, f) for f in (False, True)])
    i = jnp.arange(TILE, dtype=jnp.int32)
    tiles_c = _band_buckets(i, i, 16, False)[None]
    return tiles_a, tiles_b, tiles_c


def _diff_buckets(seq, tile):
    u = _tile_token_offsets()
    i = u[None, None, :]
    j = u[None, :, None]
    d = jnp.arange(seq // tile, dtype=jnp.int32)[:, None, None]
    dist = d * tile + i - j
    bkt = jnp.where(dist >= 0, _rel_bucket(dist), -1)
    return jnp.concatenate([bkt, jnp.full((1, tile, tile), -1, jnp.int32)], axis=0)


def _inproj_kernel(x_ref, g_ref, w_ref, wvt_ref, cs_ref, o_ref, vt_ref):
    h = _rms(x_ref[...], g_ref[...], EPS).astype(BF16)
    o_ref[...] = (_dot(h, w_ref[...]) * cs_ref[...]).astype(BF16)
    vt = _nt_dot(wvt_ref[...], h).astype(BF16)
    for j in range(vt_ref.shape[0]):
        vt_ref[j] = vt[:, j * DIFF_TILE:(j + 1) * DIFF_TILE]


def _inproj_call(x2, g, w, wvt, colscale):
    m, d = x2.shape
    n, f = w.shape[1], wvt.shape[0]
    per_step = ROW_TILE // DIFF_TILE
    out_specs = [pl.BlockSpec((ROW_TILE, n), lambda i: (i, 0)),
                 pl.BlockSpec((per_step, f, DIFF_TILE), lambda i: (i, 0, 0))]
    out_shape = [jax.ShapeDtypeStruct((m, n), BF16),
                 jax.ShapeDtypeStruct((m // DIFF_TILE, f, DIFF_TILE), BF16)]
    return pl.pallas_call(
        _inproj_kernel,
        grid=(m // ROW_TILE,),
        in_specs=[pl.BlockSpec((ROW_TILE, d), lambda i: (i, 0)),
                  pl.BlockSpec((1, d), lambda i: (0, 0)),
                  pl.BlockSpec((d, n), lambda i: (0, 0)),
                  pl.BlockSpec((f, d), lambda i: (0, 0)),
                  pl.BlockSpec((1, n), lambda i: (0, 0))],
        out_specs=out_specs,
        out_shape=out_shape,
        compiler_params=pltpu.CompilerParams(
            dimension_semantics=("parallel",), vmem_limit_bytes=VMEM_LIMIT),
        name="inproj",
    )(x2, g, w, wvt, colscale)


def _head_masks(shape, dtype):
    lane = lax.broadcasted_iota(jnp.int32, shape, 1)
    return [jnp.where(sel, 1.0, 0.0).astype(dtype) for sel in (lane < HEAD_DIM, lane >= HEAD_DIM)]


def _v_ext(v):
    rows = [jnp.concatenate([v * mk, mk], axis=-1) for mk in _head_masks(v.shape, v.dtype)]
    return jnp.concatenate(rows, axis=0)


def _attend(q, k, v, bias, m_old, accl_old):
    r, kk = q.shape[0], k.shape[0]
    ps, m_new = [], []
    for h, mk in enumerate(_head_masks(q.shape, q.dtype)):
        s = _nt_dot(q * mk, k) + bias[h]
        mx = jnp.broadcast_to(jnp.max(s, axis=-1, keepdims=True), (r, LANES))
        mn = mx if m_old is None else jnp.maximum(m_old[h], mx)
        ps.append(jnp.exp2(s - jnp.concatenate([mn] * (kk // LANES), axis=-1)).astype(BF16))
        m_new.append(mn)
    accl = _dot(jnp.concatenate(ps, axis=-1), _v_ext(v))
    if m_old is not None:
        first_head = lax.broadcasted_iota(jnp.int32, (r, LANES), 1) < HEAD_DIM
        alpha = jnp.where(first_head, jnp.exp2(m_old[0] - m_new[0]), jnp.exp2(m_old[1] - m_new[1]))
        accl = jnp.concatenate([alpha, alpha], axis=-1) * accl_old + accl
    return m_new, accl


def _dil_kernel(q_ref, k_ref, v_ref, ba_ref, bb_ref, bc_ref, o_ref, m0_ref, m1_ref, accl_ref):
    n_tiles = q_ref.shape[1] // TILE
    half = TILE // 2
    m_refs = (m0_ref, m1_ref)

    def halves(ref, n):
        x = ref[0, pl.ds(pl.multiple_of(n * TILE, TILE), TILE), :].astype(F32)
        x = x.reshape(TILE // GROUP_ROWS, 2, 8, LANES)
        return [x[:, hh].reshape(half, LANES).astype(BF16) for hh in (0, 1)]

    def whole(x0, x1):
        w = x0.shape[-1]
        parts = [x.reshape(TILE // GROUP_ROWS, 1, 8, w) for x in (x0, x1)]
        return jnp.concatenate(parts, axis=1).reshape(TILE, w)

    def pass_a(n, carry):
        qh, kh, vh = halves(q_ref, n), halves(k_ref, n), halves(v_ref, n)
        prev = jnp.maximum(n - 1, 0)
        k_prev, v_prev = halves(k_ref, prev)[1], halves(v_ref, prev)[1]
        first = jnp.where(n == 0, 1, 0)
        res = []
        for hh in (0, 1):
            variant = first if hh == 0 else 0
            res.append(_attend(qh[hh],
                               jnp.concatenate([k_prev if hh == 0 else kh[0], kh[hh]], axis=0),
                               jnp.concatenate([v_prev if hh == 0 else vh[0], vh[hh]], axis=0),
                               [ba_ref[h, variant] for h in (0, 1)], None, None))
        rows = pl.ds(pl.multiple_of(n * TILE, TILE), TILE)
        accl_ref[rows, :] = whole(res[0][1], res[1][1])
        for h in (0, 1):
            m_refs[h][rows, :] = whole(res[0][0][h], res[1][0][h])
        return carry

    lax.fori_loop(0, n_tiles, pass_a, 0, unroll=4)

    def pass_b(nb, carry):
        t0 = 2 * nb
        first = jnp.where(nb == 0, 1, 0)
        key_tiles = [jnp.maximum(t0 - 2, 0), jnp.maximum(t0 - 1, 0), t0, t0 + 1]
        for c in range(4):
            def chunks(ref, tiles, batch=None):
                parts = []
                for t in tiles:
                    rows = pl.ds(pl.multiple_of(t * TILE + c * 64, 64), 64)
                    parts.append(ref[rows, :] if batch is None else ref[batch, rows, :])
                return jnp.concatenate(parts, axis=0)
            m_new, accl = _attend(chunks(q_ref, [t0, t0 + 1], 0), chunks(k_ref, key_tiles, 0),
                                  chunks(v_ref, key_tiles, 0),
                                  [bb_ref[h, first] for h in (0, 1)],
                                  [chunks(m_refs[h], [t0, t0 + 1]) for h in (0, 1)],
                                  chunks(accl_ref, [t0, t0 + 1]))
            for i, t in enumerate((t0, t0 + 1)):
                rows = pl.ds(pl.multiple_of(t * TILE + c * 64, 64), 64)
                accl_ref[rows, :] = accl[64 * i:64 * (i + 1)]
                for h in (0, 1):
                    m_refs[h][rows, :] = m_new[h][64 * i:64 * (i + 1)]
        return carry

    lax.fori_loop(0, n_tiles // 2, pass_b, 0, unroll=2)

    def pass_c(a, carry):
        for c in range(4):
            def atoms(ref, batch=None):
                parts = []
                for t in range(n_tiles):
                    rows = pl.ds(pl.multiple_of(t * TILE + c * 64 + a * GROUP_ROWS, GROUP_ROWS),
                                 GROUP_ROWS)
                    parts.append(ref[rows, :] if batch is None else ref[batch, rows, :])
                return jnp.concatenate(parts, axis=0)
            _, accl = _attend(atoms(q_ref, 0), atoms(k_ref, 0), atoms(v_ref, 0),
                              [bc_ref[h, 0] for h in (0, 1)],
                              [atoms(m_refs[h]) for h in (0, 1)], atoms(accl_ref))
            out = (accl[:, :LANES] / accl[:, LANES:]).astype(BF16)
            for t in range(n_tiles):
                rows = pl.ds(pl.multiple_of(t * TILE + c * 64 + a * GROUP_ROWS, GROUP_ROWS),
                             GROUP_ROWS)
                o_ref[0, rows, :] = out[GROUP_ROWS * t:GROUP_ROWS * (t + 1)]
        return carry

    lax.fori_loop(0, 4, pass_c, 0, unroll=4)


def _dil_call(proj, bias_a, bias_b, bias_c):
    b, s, w = proj.shape
    assert s == TILE * GROUP_ROWS
    hw = 2 * HEAD_DIM
    n_pairs = N_HEADS_DIL // 2
    k0, v0 = DIL_WIDTH // hw, 2 * DIL_WIDTH // hw

    def bias_spec(bias):
        return pl.BlockSpec((2,) + bias.shape[1:], lambda hp, bi: (hp, 0, 0, 0))

    return pl.pallas_call(
        _dil_kernel,
        grid=(n_pairs, b),
        in_specs=[pl.BlockSpec((1, s, hw), lambda hp, bi: (bi, 0, hp)),
                  pl.BlockSpec((1, s, hw), lambda hp, bi: (bi, 0, k0 + hp)),
                  pl.BlockSpec((1, s, hw), lambda hp, bi: (bi, 0, v0 + hp)),
                  bias_spec(bias_a), bias_spec(bias_b), bias_spec(bias_c)],
        out_specs=pl.BlockSpec((1, s, hw), lambda hp, bi: (bi, 0, hp)),
        out_shape=jax.ShapeDtypeStruct((b, s, DIL_WIDTH), BF16),
        scratch_shapes=[pltpu.VMEM((s, LANES), F32),
                        pltpu.VMEM((s, LANES), F32),
                        pltpu.VMEM((s, 2 * LANES), F32)],
        compiler_params=pltpu.CompilerParams(
            dimension_semantics=("parallel", "parallel"), vmem_limit_bytes=VMEM_LIMIT),
        name="dilated_attn",
    )(proj, proj, proj, bias_a, bias_b, bias_c)


def _diff_kernel(lq_ref, q_ref, k_ref, vt_ref, bias_ref, g_ref, o_ref,
                 acc_ref, sa_ref, sb_ref, *, lam_init):
    t = DIFF_TILE
    hw = 2 * HEAD_DIM
    n_delta = bias_ref.shape[1] - 1
    ones = jnp.ones((ONES_ROWS, t), BF16)
    lq = lq_ref[0]
    lam = (jnp.exp(jnp.sum(lq[0:1] * lq[1:2], axis=-1, keepdims=True))
           - jnp.exp(jnp.sum(lq[2:3] * lq[3:4], axis=-1, keepdims=True)) + lam_init)

    n_tiles = q_ref.shape[1] // t
    streams = range(q_ref.shape[0])

    def q_block(qi, carry):
        rows = pl.ds(pl.multiple_of(qi * t, t), t)
        qq = []
        for e in streams:
            q = q_ref[e, rows, :]
            lane = lax.broadcasted_iota(jnp.int32, q.shape, 1)
            zero = jnp.zeros_like(q)
            qq.append(jnp.concatenate([jnp.where(lane < HEAD_DIM, q, zero),
                                       jnp.where(lane >= HEAD_DIM, q, zero)], axis=0))
        acc_ref[...] = jnp.zeros(acc_ref.shape, F32)

        def scores(e, kj, bias):
            off = pl.multiple_of(jnp.minimum(kj, qi) * t, t)
            return _nt_dot(k_ref[e, pl.ds(off, t), :], qq[e]) + bias

        def bias_of(kj):
            bias = bias_ref[0, jnp.where(kj > qi, n_delta, qi - kj)]
            return jnp.concatenate([bias, bias], axis=-1)

        def accumulate(e, s, kj, m_prev):
            vt_ext = jnp.concatenate([vt_ref[e * n_tiles + jnp.minimum(kj, qi)], ones], axis=0)
            m_next = jnp.maximum(m_prev, jnp.max(s, axis=0, keepdims=True))
            p = jnp.exp2(s - m_next).astype(BF16)
            acc_ref[e] = jnp.exp2(m_prev - m_next) * acc_ref[e] + _dot(vt_ext, p)
            return m_next

        bias0 = bias_of(0)
        for e in streams:
            sa_ref[e] = scores(e, 0, bias0)

        def body(jj, ms):
            k0 = 2 * jj
            bias = bias_of(k0 + 1)
            for e in streams:
                sb_ref[e] = scores(e, k0 + 1, bias)
            ms = [accumulate(e, sa_ref[e], k0, ms[e]) for e in streams]
            bias = bias_of(k0 + 2)
            for e in streams:
                sa_ref[e] = scores(e, k0 + 2, bias)
            return [accumulate(e, sb_ref[e], k0 + 1, ms[e]) for e in streams]

        lax.fori_loop(0, (qi + 2) // 2, body, [jnp.full((1, 2 * t), NEG, F32) for _ in streams])

        for e in streams:
            acc = acc_ref[e]
            o = acc[:hw] / acc[hw:hw + 1]
            a = (o[:, :t] - lam * o[:, t:]).T
            o_ref[e, rows, :] = (_rms(a, g_ref[0], SUBLN_EPS) * (1.0 - lam_init)).astype(BF16)
        return carry

    lax.fori_loop(0, n_tiles, q_block, 0)


def _diff_call(proj, vt, bias, lq, g, *, lam_init):
    b, s, w = proj.shape
    t = DIFF_TILE
    hw = 2 * HEAD_DIM
    q0 = 3 * DIL_WIDTH // hw
    k0 = q0 + N_HEADS_DIFF
    ns = DIFF_STREAMS
    assert b % ns == 0
    return pl.pallas_call(
        functools.partial(_diff_kernel, lam_init=lam_init),
        grid=(N_HEADS_DIFF, b // ns),
        in_specs=[pl.BlockSpec((1, 4, HEAD_DIM), lambda h, bi: (0, 0, 0)),
                  pl.BlockSpec((ns, s, hw), lambda h, bi: (bi, 0, q0 + h)),
                  pl.BlockSpec((ns, s, hw), lambda h, bi: (bi, 0, k0 + h)),
                  pl.BlockSpec((ns * (s // t), hw, t), lambda h, bi: (bi, h, 0)),
                  pl.BlockSpec((1, s // t + 1, t, t), lambda h, bi: (h, 0, 0, 0)),
                  pl.BlockSpec((1, 1, hw), lambda h, bi: (0, 0, 0))],
        out_specs=pl.BlockSpec((ns, s, hw), lambda h, bi: (bi, 0, h)),
        out_shape=jax.ShapeDtypeStruct((b, s, DIFF_WIDTH), BF16),
        scratch_shapes=[pltpu.VMEM((ns, hw + ONES_ROWS, 2 * t), F32),
                        pltpu.VMEM((ns, t, 2 * t), F32),
                        pltpu.VMEM((ns, t, 2 * t), F32)],
        compiler_params=pltpu.CompilerParams(
            dimension_semantics=("parallel", "parallel"), vmem_limit_bytes=VMEM_LIMIT),
        name="diff_attn",
    )(lq, proj, proj, vt, bias, g)


def _ff_chunks(d_ff):
    chunks, c0 = [], 0
    while c0 < d_ff:
        cw = min(FF_CHUNK, d_ff - c0)
        chunks.append((c0, cw))
        c0 += cw
    return chunks


def _mlp_kernel(*refs, final):
    x_ref, oa_ref, ob_ref, wo_ref, g_ref, wgu_ref, wd_ref = refs[:7]
    o_ref, act_ref = refs[-2:]
    d_ff = wd_ref.shape[0]
    x = (x_ref[...] + _dot(oa_ref[...], wo_ref[:DIL_WIDTH, :])
         + _dot(ob_ref[...], wo_ref[DIL_WIDTH:, :]))
    h = _rms(x, g_ref[...], EPS).astype(BF16)
    for c0, cw in _ff_chunks(d_ff):
        gate = _dot(h, wgu_ref[:, c0:c0 + cw])
        up = _dot(h, wgu_ref[:, d_ff + c0:d_ff + c0 + cw])
        act_ref[:, c0:c0 + cw] = ((gate * jax.nn.sigmoid(gate)) * up).astype(BF16)
    x = x + _dot(act_ref[...], wd_ref[...])
    if final:
        x = _rms(x, refs[7][...], EPS)
    o_ref[...] = x


def _mlp_call(x2, oa, ob, wo, g, wgu, wd, g_final):
    m, d = x2.shape
    d_ff = wd.shape[0]
    final = g_final is not None
    row = lambda i: (i, 0)
    fixed = lambda i: (0, 0)
    in_specs = [pl.BlockSpec((ROW_TILE, d), row),
                pl.BlockSpec((ROW_TILE, DIL_WIDTH), row),
                pl.BlockSpec((ROW_TILE, DIFF_WIDTH), row),
                pl.BlockSpec((MIX_WIDTH, d), fixed, pipeline_mode=pl.Buffered(1)),
                pl.BlockSpec((1, d), fixed),
                pl.BlockSpec((d, 2 * d_ff), fixed, pipeline_mode=pl.Buffered(1)),
                pl.BlockSpec((d_ff, d), fixed, pipeline_mode=pl.Buffered(1))]
    args = [x2, oa, ob, wo, g, wgu, wd]
    if final:
        in_specs.append(pl.BlockSpec((1, d), fixed))
        args.append(g_final)
    return pl.pallas_call(
        functools.partial(_mlp_kernel, final=final),
        grid=(m // ROW_TILE,),
        in_specs=in_specs,
        out_specs=pl.BlockSpec((ROW_TILE, d), row),
        out_shape=jax.ShapeDtypeStruct((m, d), F32),
        scratch_shapes=[pltpu.VMEM((ROW_TILE, d_ff), BF16)],
        compiler_params=pltpu.CompilerParams(
            dimension_semantics=("parallel",), vmem_limit_bytes=VMEM_LIMIT),
        name="outproj_mlp",
    )(*args)


def kernel(x, g_attn, w_in, w_out, rel_bias, lambda_qk, subln_g, g_ffn,
           w_gate_up, w_down, g_final):
    b, s, d = x.shape
    depth = w_in.shape[0]
    assert (b * s) % ROW_TILE == 0
    nt = s // TILE

    bias_a, bias_b, bias_c = (_bias_call(rel_bias, bk, 0, N_HEADS_DIL) for bk in _dilated_buckets())
    bias_diff = _bias_call(rel_bias, _diff_buckets(s, DIFF_TILE), N_HEADS_DIL, N_HEADS_DIFF)

    col = jnp.arange(w_in.shape[-1] - DIFF_WIDTH) // DIL_WIDTH
    qscale = jnp.where((col == 0) | (col == 3), LOG2E * HEAD_DIM ** -0.5, 1.0).astype(F32)[None]

    x2 = x.reshape(b, nt, GROUP_ROWS, 4, 4, d).transpose(0, 1, 4, 3, 2, 5).reshape(b * s, d)
    for l in range(depth):
        lam_init = 0.8 - 0.6 * math.exp(-0.3 * l)
        w_in_l = w_in[l].astype(BF16)
        n_tok = w_in_l.shape[1] - DIFF_WIDTH
        proj, vt = _inproj_call(x2, g_attn[l][None], w_in_l[:, :n_tok], w_in_l[:, n_tok:].T, qscale)
        proj = proj.reshape(b, s, -1)
        oa = _dil_call(proj, bias_a, bias_b, bias_c)
        ob = _diff_call(proj, vt, bias_diff, lambda_qk[l][None], subln_g[l][None, None],
                        lam_init=lam_init)
        x2 = _mlp_call(x2, oa.reshape(b * s, DIL_WIDTH), ob.reshape(b * s, DIFF_WIDTH),
                       w_out[l].astype(BF16), g_ffn[l][None],
                       w_gate_up[l].astype(BF16), w_down[l].astype(BF16),
                       g_final[None] if l == depth - 1 else None)
    return x2.reshape(b, nt, 4, 4, GROUP_ROWS, d).transpose(0, 1, 4, 3, 2, 5).reshape(b, s, d)
```

```python
import functools
import math

import jax
import jax.numpy as jnp
from jax import lax
from jax.experimental import pallas as pl
from jax.experimental.pallas import tpu as pltpu

F32 = jnp.float32
BF16 = jnp.bfloat16

HEAD_DIM = 64
N_HEADS_DIL = 8
N_HEADS_DIFF = 4
DIL_WIDTH = N_HEADS_DIL * HEAD_DIM
DIFF_WIDTH = N_HEADS_DIFF * 2 * HEAD_DIM
MIX_WIDTH = DIL_WIDTH + DIFF_WIDTH
BAND = 128
N_BUCKETS = 32
MAX_DISTANCE = 2048
EPS = 1e-6
SUBLN_EPS = 1e-5
NEG = -1e30
LOG2E = math.log2(math.e)

LANES = 128
VMEM_LIMIT = 56 * 1024 * 1024
ROW_TILE = 512
TILE = 256
GROUP_ROWS = 16
PERM_ROWS = 2048
DIFF_TILE = TILE
FF_CHUNK = 768
ONES_ROWS = 16
DIFF_STREAMS = 2


def _nt_dot(a, b):
    return lax.dot_general(a, b, (((1,), (1,)), ((), ())), preferred_element_type=F32)


def _dot(a, b):
    return jnp.dot(a, b, preferred_element_type=F32)


def _rms(x, g, eps):
    return (x * lax.rsqrt(jnp.mean(x * x, axis=-1, keepdims=True) + eps)) * g


def _rel_bucket(dist):
    max_exact = N_BUCKETS // 2
    n = jnp.maximum(dist, 0)
    nf = jnp.maximum(n, 1).astype(F32)
    large = max_exact + (jnp.log(nf / max_exact) / math.log(MAX_DISTANCE / max_exact)
                         * (N_BUCKETS - max_exact)).astype(jnp.int32)
    large = jnp.minimum(large, N_BUCKETS - 1)
    return jnp.where(n < max_exact, n, large)


def _bias_kernel(tab_ref, bkt_ref, o_ref, *, head0):
    h = pl.program_id(0) + head0
    bkt = bkt_ref[0]
    out = jnp.full(bkt.shape, NEG, F32)
    for b in range(N_BUCKETS):
        out = jnp.where(bkt == b, tab_ref[b, h] * LOG2E, out)
    o_ref[0, 0] = out


def _bias_call(rel_bias, buckets, head0, n_heads):
    nt, r, c = buckets.shape
    return pl.pallas_call(
        functools.partial(_bias_kernel, head0=head0),
        grid=(n_heads, nt),
        in_specs=[pl.BlockSpec(memory_space=pltpu.SMEM),
                  pl.BlockSpec((1, r, c), lambda h, t: (t, 0, 0))],
        out_specs=pl.BlockSpec((1, 1, r, c), lambda h, t: (h, t, 0, 0)),
        out_shape=jax.ShapeDtypeStruct((n_heads, nt, r, c), F32),
        name="bias_tiles",
    )(rel_bias, buckets)


def _tile_token_offsets():
    pos = jnp.arange(TILE, dtype=jnp.int32)
    c, a, il = pos // 64, (pos // GROUP_ROWS) % 4, pos % GROUP_ROWS
    return GROUP_ROWS * il + 4 * a + c


def _band_buckets(mq, mk, dil, first):
    m = mq[:, None] - mk[None, :]
    valid = (m >= 0) & (m <= BAND)
    if first:
        valid = valid & (mk[None, :] >= 0)
    return jnp.where(valid, _rel_bucket(m * dil), -1)


def _dilated_buckets():
    half = TILE // 2
    r = jnp.arange(half, dtype=jnp.int32)
    g, il_lo = r // 8, r % 8
    w = GROUP_ROWS * il_lo + 4 * (g % 4) + g // 4
    mk = jnp.concatenate([w - half, w])
    tiles_a = jnp.stack([_band_buckets(w, mk, 1, f) for f in (False, True)])
    r = jnp.arange(64, dtype=jnp.int32)
    j = 4 * (r % GROUP_ROWS) + r // GROUP_ROWS
    mq = jnp.concatenate([j, j + 64])
    mk = jnp.concatenate([j - 128, j - 64, j, j + 64])
    tiles_b = jnp.stack([_band_buckets(mq, mk, 4, f) for f in (False, True)])
    i = jnp.arange(TILE, dtype=jnp.int32)
    tiles_c = _band_buckets(i, i, 16, False)[None]
    return tiles_a, tiles_b, tiles_c


def _diff_buckets(seq, tile):
    u = _tile_token_offsets()
    i = u[None, None, :]
    j = u[None, :, None]
    d = jnp.arange(seq // tile, dtype=jnp.int32)[:, None, None]
    dist = d * tile + i - j
    bkt = jnp.where(dist >= 0, _rel_bucket(dist), -1)
    return jnp.concatenate([bkt, jnp.full((1, tile, tile), -1, jnp.int32)], axis=0)


def _permute_kernel(x_ref, o_ref, *, inverse):
    for tt in range(PERM_ROWS // TILE):
        for c in range(4):
            for a in range(4):
                nat = pl.ds(tt * TILE + 4 * a + c, GROUP_ROWS, stride=GROUP_ROWS)
                stored = pl.ds(tt * TILE + 64 * c + GROUP_ROWS * a, GROUP_ROWS)
                if inverse:
                    o_ref[nat, :] = x_ref[stored, :]
                else:
                    o_ref[stored, :] = x_ref[nat, :]


def _permute_call(x2, inverse):
    m, d = x2.shape
    return pl.pallas_call(
        functools.partial(_permute_kernel, inverse=inverse),
        grid=(m // PERM_ROWS, d // LANES),
        in_specs=[pl.BlockSpec((PERM_ROWS, LANES), lambda i, j: (i, j))],
        out_specs=pl.BlockSpec((PERM_ROWS, LANES), lambda i, j: (i, j)),
        out_shape=jax.ShapeDtypeStruct((m, d), x2.dtype),
        compiler_params=pltpu.CompilerParams(dimension_semantics=("parallel", "parallel")),
        name="permute",
    )(x2)


def _inproj_kernel(x_ref, g_ref, w_ref, wvt_ref, cs_ref, o_ref, vt_ref):
    h = _rms(x_ref[...], g_ref[...], EPS).astype(BF16)
    o_ref[...] = (_dot(h, w_ref[...]) * cs_ref[...]).astype(BF16)
    vt = _nt_dot(wvt_ref[...], h).astype(BF16)
    for j in range(vt_ref.shape[0]):
        vt_ref[j] = vt[:, j * DIFF_TILE:(j + 1) * DIFF_TILE]


def _inproj_call(x2, g, w, wvt, colscale):
    m, d = x2.shape
    n, f = w.shape[1], wvt.shape[0]
    per_step = ROW_TILE // DIFF_TILE
    out_specs = [pl.BlockSpec((ROW_TILE, n), lambda i: (i, 0)),
                 pl.BlockSpec((per_step, f, DIFF_TILE), lambda i: (i, 0, 0))]
    out_shape = [jax.ShapeDtypeStruct((m, n), BF16),
                 jax.ShapeDtypeStruct((m // DIFF_TILE, f, DIFF_TILE), BF16)]
    return pl.pallas_call(
        _inproj_kernel,
        grid=(m // ROW_TILE,),
        in_specs=[pl.BlockSpec((ROW_TILE, d), lambda i: (i, 0)),
                  pl.BlockSpec((1, d), lambda i: (0, 0)),
                  pl.BlockSpec((d, n), lambda i: (0, 0)),
                  pl.BlockSpec((f, d), lambda i: (0, 0)),
                  pl.BlockSpec((1, n), lambda i: (0, 0))],
        out_specs=out_specs,
        out_shape=out_shape,
        compiler_params=pltpu.CompilerParams(
            dimension_semantics=("parallel",), vmem_limit_bytes=VMEM_LIMIT),
        name="inproj",
    )(x2, g, w, wvt, colscale)


def _head_masks(shape, dtype):
    lane = lax.broadcasted_iota(jnp.int32, shape, 1)
    return [jnp.where(sel, 1.0, 0.0).astype(dtype) for sel in (lane < HEAD_DIM, lane >= HEAD_DIM)]


def _v_ext(v):
    rows = [jnp.concatenate([v * mk, mk], axis=-1) for mk in _head_masks(v.shape, v.dtype)]
    return jnp.concatenate(rows, axis=0)


def _attend(q, k, v, bias, m_old, accl_old):
    r, kk = q.shape[0], k.shape[0]
    ps, m_new = [], []
    for h, mk in enumerate(_head_masks(q.shape, q.dtype)):
        s = _nt_dot(q * mk, k) + bias[h]
        mx = jnp.broadcast_to(jnp.max(s, axis=-1, keepdims=True), (r, LANES))
        mn = mx if m_old is None else jnp.maximum(m_old[h], mx)
        ps.append(jnp.exp2(s - jnp.concatenate([mn] * (kk // LANES), axis=-1)).astype(BF16))
        m_new.append(mn)
    accl = _dot(jnp.concatenate(ps, axis=-1), _v_ext(v))
    if m_old is not None:
        first_head = lax.broadcasted_iota(jnp.int32, (r, LANES), 1) < HEAD_DIM
        alpha = jnp.where(first_head, jnp.exp2(m_old[0] - m_new[0]), jnp.exp2(m_old[1] - m_new[1]))
        accl = jnp.concatenate([alpha, alpha], axis=-1) * accl_old + accl
    return m_new, accl


def _dil_kernel(q_ref, k_ref, v_ref, ba_ref, bb_ref, bc_ref, o_ref, m0_ref, m1_ref, accl_ref):
    n_tiles = q_ref.shape[1] // TILE
    half = TILE // 2
    m_refs = (m0_ref, m1_ref)

    def halves(ref, n):
        x = ref[0, pl.ds(pl.multiple_of(n * TILE, TILE), TILE), :].astype(F32)
        x = x.reshape(TILE // GROUP_ROWS, 2, 8, LANES)
        return [x[:, hh].reshape(half, LANES).astype(BF16) for hh in (0, 1)]

    def whole(x0, x1):
        w = x0.shape[-1]
        parts = [x.reshape(TILE // GROUP_ROWS, 1, 8, w) for x in (x0, x1)]
        return jnp.concatenate(parts, axis=1).reshape(TILE, w)

    def pass_a(n, carry):
        qh, kh, vh = halves(q_ref, n), halves(k_ref, n), halves(v_ref, n)
        prev = jnp.maximum(n - 1, 0)
        k_prev, v_prev = halves(k_ref, prev)[1], halves(v_ref, prev)[1]
        first = jnp.where(n == 0, 1, 0)
        res = []
        for hh in (0, 1):
            variant = first if hh == 0 else 0
            res.append(_attend(qh[hh],
                               jnp.concatenate([k_prev if hh == 0 else kh[0], kh[hh]], axis=0),
                               jnp.concatenate([v_prev if hh == 0 else vh[0], vh[hh]], axis=0),
                               [ba_ref[h, variant] for h in (0, 1)], None, None))
        rows = pl.ds(pl.multiple_of(n * TILE, TILE), TILE)
        accl_ref[rows, :] = whole(res[0][1], res[1][1])
        for h in (0, 1):
            m_refs[h][rows, :] = whole(res[0][0][h], res[1][0][h])
        return carry

    lax.fori_loop(0, n_tiles, pass_a, 0, unroll=4)

    def pass_b(nb, carry):
        t0 = 2 * nb
        first = jnp.where(nb == 0, 1, 0)
        key_tiles = [jnp.maximum(t0 - 2, 0), jnp.maximum(t0 - 1, 0), t0, t0 + 1]
        for c in range(4):
            def chunks(ref, tiles, batch=None):
                parts = []
                for t in tiles:
                    rows = pl.ds(pl.multiple_of(t * TILE + c * 64, 64), 64)
                    parts.append(ref[rows, :] if batch is None else ref[batch, rows, :])
                return jnp.concatenate(parts, axis=0)
            m_new, accl = _attend(chunks(q_ref, [t0, t0 + 1], 0), chunks(k_ref, key_tiles, 0),
                                  chunks(v_ref, key_tiles, 0),
                                  [bb_ref[h, first] for h in (0, 1)],
                                  [chunks(m_refs[h], [t0, t0 + 1]) for h in (0, 1)],
                                  chunks(accl_ref, [t0, t0 + 1]))
            for i, t in enumerate((t0, t0 + 1)):
                rows = pl.ds(pl.multiple_of(t * TILE + c * 64, 64), 64)
                accl_ref[rows, :] = accl[64 * i:64 * (i + 1)]
                for h in (0, 1):
                    m_refs[h][rows, :] = m_new[h][64 * i:64 * (i + 1)]
        return carry

    lax.fori_loop(0, n_tiles // 2, pass_b, 0, unroll=2)

    def pass_c(a, carry):
        for c in range(4):
            def atoms(ref, batch=None):
                parts = []
                for t in range(n_tiles):
                    rows = pl.ds(pl.multiple_of(t * TILE + c * 64 + a * GROUP_ROWS, GROUP_ROWS),
                                 GROUP_ROWS)
                    parts.append(ref[rows, :] if batch is None else ref[batch, rows, :])
                return jnp.concatenate(parts, axis=0)
            _, accl = _attend(atoms(q_ref, 0), atoms(k_ref, 0), atoms(v_ref, 0),
                              [bc_ref[h, 0] for h in (0, 1)],
                              [atoms(m_refs[h]) for h in (0, 1)], atoms(accl_ref))
            out = (accl[:, :LANES] / accl[:, LANES:]).astype(BF16)
            for t in range(n_tiles):
                rows = pl.ds(pl.multiple_of(t * TILE + c * 64 + a * GROUP_ROWS, GROUP_ROWS),
                             GROUP_ROWS)
                o_ref[0, rows, :] = out[GROUP_ROWS * t:GROUP_ROWS * (t + 1)]
        return carry

    lax.fori_loop(0, 4, pass_c, 0, unroll=4)


def _dil_call(proj, bias_a, bias_b, bias_c):
    b, s, w = proj.shape
    assert s == TILE * GROUP_ROWS
    hw = 2 * HEAD_DIM
    n_pairs = N_HEADS_DIL // 2
    k0, v0 = DIL_WIDTH // hw, 2 * DIL_WIDTH // hw

    def bias_spec(bias):
        return pl.BlockSpec((2,) + bias.shape[1:], lambda hp, bi: (hp, 0, 0, 0))

    return pl.pallas_call(
        _dil_kernel,
        grid=(n_pairs, b),
        in_specs=[pl.BlockSpec((1, s, hw), lambda hp, bi: (bi, 0, hp)),
                  pl.BlockSpec((1, s, hw), lambda hp, bi: (bi, 0, k0 + hp)),
                  pl.BlockSpec((1, s, hw), lambda hp, bi: (bi, 0, v0 + hp)),
                  bias_spec(bias_a), bias_spec(bias_b), bias_spec(bias_c)],
        out_specs=pl.BlockSpec((1, s, hw), lambda hp, bi: (bi, 0, hp)),
        out_shape=jax.ShapeDtypeStruct((b, s, DIL_WIDTH), BF16),
        scratch_shapes=[pltpu.VMEM((s, LANES), F32),
                        pltpu.VMEM((s, LANES), F32),
                        pltpu.VMEM((s, 2 * LANES), F32)],
        compiler_params=pltpu.CompilerParams(
            dimension_semantics=("parallel", "parallel"), vmem_limit_bytes=VMEM_LIMIT),
        name="dilated_attn",
    )(proj, proj, proj, bias_a, bias_b, bias_c)


def _diff_kernel(lq_ref, q_ref, k_ref, vt_ref, bias_ref, g_ref, o_ref,
                 acc_ref, sa_ref, sb_ref, qq_ref, *, lam_init):
    t = DIFF_TILE
    hw = 2 * HEAD_DIM
    n_delta = bias_ref.shape[1] - 1
    ones = jnp.ones((ONES_ROWS, t), BF16)
    lq = lq_ref[0]
    lam = (jnp.exp(jnp.sum(lq[0:1] * lq[1:2], axis=-1, keepdims=True))
           - jnp.exp(jnp.sum(lq[2:3] * lq[3:4], axis=-1, keepdims=True)) + lam_init)

    n_tiles = q_ref.shape[1] // t
    streams = range(q_ref.shape[0])

    def load_queries(qi):
        rows = pl.ds(pl.multiple_of(qi * t, t), t)
        bias0 = bias_ref[0, qi]
        bias0 = jnp.concatenate([bias0, bias0], axis=-1)
        for e in streams:
            q = q_ref[e, rows, :]
            lane = lax.broadcasted_iota(jnp.int32, q.shape, 1)
            zero = jnp.zeros_like(q)
            qq = jnp.concatenate([jnp.where(lane < HEAD_DIM, q, zero),
                                  jnp.where(lane >= HEAD_DIM, q, zero)], axis=0)
            qq_ref[e] = qq
            sa_ref[e] = _nt_dot(k_ref[e, pl.ds(0, t), :], qq) + bias0

    def q_block(qi, carry):
        def scores(e, kj, bias):
            off = pl.multiple_of(jnp.minimum(kj, qi) * t, t)
            return _nt_dot(k_ref[e, pl.ds(off, t), :], qq_ref[e]) + bias

        def bias_of(kj):
            bias = bias_ref[0, jnp.where(kj > qi, n_delta, qi - kj)]
            return jnp.concatenate([bias, bias], axis=-1)

        def accumulate(e, s, kj, m_prev):
            vt_ext = jnp.concatenate([vt_ref[e * n_tiles + jnp.minimum(kj, qi)], ones], axis=0)
            m_next = jnp.maximum(m_prev, jnp.max(s, axis=0, keepdims=True))
            p = jnp.exp2(s - m_next).astype(BF16)
            acc_ref[e] = jnp.exp2(m_prev - m_next) * acc_ref[e] + _dot(vt_ext, p)
            return m_next

        def body(jj, ms):
            k0 = 2 * jj
            bias = bias_of(k0 + 1)
            for e in streams:
                sb_ref[e] = scores(e, k0 + 1, bias)
            ms = [accumulate(e, sa_ref[e], k0, ms[e]) for e in streams]
            bias = bias_of(k0 + 2)
            for e in streams:
                sa_ref[e] = scores(e, k0 + 2, bias)
            return [accumulate(e, sb_ref[e], k0 + 1, ms[e]) for e in streams]

        lax.fori_loop(0, (qi + 2) // 2, body, [jnp.full((1, 2 * t), NEG, F32) for _ in streams])

        rows = pl.ds(pl.multiple_of(qi * t, t), t)
        for e in streams:
            acc = acc_ref[e]
            o = acc[:hw] / acc[hw:hw + 1]
            a = (o[:, :t] - lam * o[:, t:]).T
            o_ref[e, rows, :] = (_rms(a, g_ref[0], SUBLN_EPS) * (1.0 - lam_init)).astype(BF16)
        load_queries(jnp.minimum(qi + 1, n_tiles - 1))
        return carry

    acc_ref[...] = jnp.zeros(acc_ref.shape, F32)
    load_queries(0)
    lax.fori_loop(0, n_tiles, q_block, 0)


def _diff_call(proj, vt, bias, lq, g, *, lam_init):
    b, s, w = proj.shape
    t = DIFF_TILE
    hw = 2 * HEAD_DIM
    q0 = 3 * DIL_WIDTH // hw
    k0 = q0 + N_HEADS_DIFF
    ns = DIFF_STREAMS
    assert b % ns == 0
    return pl.pallas_call(
        functools.partial(_diff_kernel, lam_init=lam_init),
        grid=(N_HEADS_DIFF, b // ns),
        in_specs=[pl.BlockSpec((1, 4, HEAD_DIM), lambda h, bi: (0, 0, 0)),
                  pl.BlockSpec((ns, s, hw), lambda h, bi: (bi, 0, q0 + h)),
                  pl.BlockSpec((ns, s, hw), lambda h, bi: (bi, 0, k0 + h)),
                  pl.BlockSpec((ns * (s // t), hw, t), lambda h, bi: (bi, h, 0)),
                  pl.BlockSpec((1, s // t + 1, t, t), lambda h, bi: (h, 0, 0, 0)),
                  pl.BlockSpec((1, 1, hw), lambda h, bi: (0, 0, 0))],
        out_specs=pl.BlockSpec((ns, s, hw), lambda h, bi: (bi, 0, h)),
        out_shape=jax.ShapeDtypeStruct((b, s, DIFF_WIDTH), BF16),
        scratch_shapes=[pltpu.VMEM((ns, hw + ONES_ROWS, 2 * t), F32),
                        pltpu.VMEM((ns, t, 2 * t), F32),
                        pltpu.VMEM((ns, t, 2 * t), F32),
                        pltpu.VMEM((ns, 2 * t, hw), BF16)],
        compiler_params=pltpu.CompilerParams(
            dimension_semantics=("parallel", "parallel"), vmem_limit_bytes=VMEM_LIMIT),
        name="diff_attn",
    )(lq, proj, proj, vt, bias, g)


def _ff_chunks(d_ff):
    chunks, c0 = [], 0
    while c0 < d_ff:
        cw = min(FF_CHUNK, d_ff - c0)
        chunks.append((c0, cw))
        c0 += cw
    return chunks


def _mlp_kernel(*refs, final):
    x_ref, oa_ref, ob_ref, wo_ref, g_ref, wgu_ref, wd_ref = refs[:7]
    o_ref, act_ref = refs[-2:]
    d_ff = wd_ref.shape[0]
    x = (x_ref[...] + _dot(oa_ref[...], wo_ref[:DIL_WIDTH, :])
         + _dot(ob_ref[...], wo_ref[DIL_WIDTH:, :]))
    h = _rms(x, g_ref[...], EPS).astype(BF16)
    for c0, cw in _ff_chunks(d_ff):
        gate = _dot(h, wgu_ref[:, c0:c0 + cw])
        up = _dot(h, wgu_ref[:, d_ff + c0:d_ff + c0 + cw])
        act_ref[:, c0:c0 + cw] = ((gate * jax.nn.sigmoid(gate)) * up).astype(BF16)
    x = x + _dot(act_ref[...], wd_ref[...])
    if final:
        x = _rms(x, refs[7][...], EPS)
    o_ref[...] = x


def _mlp_call(x2, oa, ob, wo, g, wgu, wd, g_final):
    m, d = x2.shape
    d_ff = wd.shape[0]
    final = g_final is not None
    row = lambda i: (i, 0)
    fixed = lambda i: (0, 0)
    in_specs = [pl.BlockSpec((ROW_TILE, d), row),
                pl.BlockSpec((ROW_TILE, DIL_WIDTH), row),
                pl.BlockSpec((ROW_TILE, DIFF_WIDTH), row),
                pl.BlockSpec((MIX_WIDTH, d), fixed, pipeline_mode=pl.Buffered(1)),
                pl.BlockSpec((1, d), fixed),
                pl.BlockSpec((d, 2 * d_ff), fixed, pipeline_mode=pl.Buffered(1)),
                pl.BlockSpec((d_ff, d), fixed, pipeline_mode=pl.Buffered(1))]
    args = [x2, oa, ob, wo, g, wgu, wd]
    if final:
        in_specs.append(pl.BlockSpec((1, d), fixed))
        args.append(g_final)
    return pl.pallas_call(
        functools.partial(_mlp_kernel, final=final),
        grid=(m // ROW_TILE,),
        in_specs=in_specs,
        out_specs=pl.BlockSpec((ROW_TILE, d), row),
        out_shape=jax.ShapeDtypeStruct((m, d), F32),
        scratch_shapes=[pltpu.VMEM((ROW_TILE, d_ff), BF16)],
        compiler_params=pltpu.CompilerParams(
            dimension_semantics=("parallel",), vmem_limit_bytes=VMEM_LIMIT),
        name="outproj_mlp",
    )(*args)


def kernel(x, g_attn, w_in, w_out, rel_bias, lambda_qk, subln_g, g_ffn,
           w_gate_up, w_down, g_final):
    b, s, d = x.shape
    depth = w_in.shape[0]
    assert (b * s) % PERM_ROWS == 0 and PERM_ROWS % ROW_TILE == 0 and d % LANES == 0

    bias_a, bias_b, bias_c = (_bias_call(rel_bias, bk, 0, N_HEADS_DIL) for bk in _dilated_buckets())
    bias_diff = _bias_call(rel_bias, _diff_buckets(s, DIFF_TILE), N_HEADS_DIL, N_HEADS_DIFF)

    col = jnp.arange(w_in.shape[-1] - DIFF_WIDTH) // DIL_WIDTH
    qscale = jnp.where((col == 0) | (col == 3), LOG2E * HEAD_DIM ** -0.5, 1.0).astype(F32)[None]

    x2 = _permute_call(x.reshape(b * s, d), inverse=False)
    for l in range(depth):
        lam_init = 0.8 - 0.6 * math.exp(-0.3 * l)
        w_in_l = w_in[l].astype(BF16)
        n_tok = w_in_l.shape[1] - DIFF_WIDTH
        proj, vt = _inproj_call(x2, g_attn[l][None], w_in_l[:, :n_tok], w_in_l[:, n_tok:].T, qscale)
        proj = proj.reshape(b, s, -1)
        oa = _dil_call(proj, bias_a, bias_b, bias_c)
        ob = _diff_call(proj, vt, bias_diff, lambda_qk[l][None], subln_g[l][None, None],
                        lam_init=lam_init)
        x2 = _mlp_call(x2, oa.reshape(b * s, DIL_WIDTH), ob.reshape(b * s, DIFF_WIDTH),
                       w_out[l].astype(BF16), g_ffn[l][None],
                       w_gate_up[l].astype(BF16), w_down[l].astype(BF16),
                       g_final[None] if l == depth - 1 else None)
    return _permute_call(x2, inverse=True).reshape(b, s, d)
```

```python
import functools
import math

import jax
import jax.numpy as jnp
from jax import lax
from jax.experimental import pallas as pl
from jax.experimental.pallas import tpu as pltpu

F32 = jnp.float32
BF16 = jnp.bfloat16

HEAD_DIM = 64
N_HEADS_DIL = 8
N_HEADS_DIFF = 4
DIL_WIDTH = N_HEADS_DIL * HEAD_DIM
DIFF_WIDTH = N_HEADS_DIFF * 2 * HEAD_DIM
MIX_WIDTH = DIL_WIDTH + DIFF_WIDTH
BAND = 128
N_BUCKETS = 32
MAX_DISTANCE = 2048
EPS = 1e-6
SUBLN_EPS = 1e-5
NEG = -1e30
LOG2E = math.log2(math.e)

LANES = 128
VMEM_LIMIT = 56 * 1024 * 1024
ROW_TILE = 1024
TILE = 256
GROUP_ROWS = 16
PERM_ROWS = 8192
DIFF_TILE = TILE
FF_CHUNK = 768
ONES_ROWS = 16
DIFF_STREAMS = 2


def _nt_dot(a, b):
    return lax.dot_general(a, b, (((1,), (1,)), ((), ())), preferred_element_type=F32)


def _dot(a, b):
    return jnp.dot(a, b, preferred_element_type=F32)


def _rms(x, g, eps):
    return (x * lax.rsqrt(jnp.mean(x * x, axis=-1, keepdims=True) + eps)) * g


def _rel_bucket(dist):
    max_exact = N_BUCKETS // 2
    n = jnp.maximum(dist, 0)
    nf = jnp.maximum(n, 1).astype(F32)
    large = max_exact + (jnp.log(nf / max_exact) / math.log(MAX_DISTANCE / max_exact)
                         * (N_BUCKETS - max_exact)).astype(jnp.int32)
    large = jnp.minimum(large, N_BUCKETS - 1)
    return jnp.where(n < max_exact, n, large)


def _bias_kernel(tab_ref, bkt_ref, o_ref, *, head0):
    h = pl.program_id(0) + head0
    bkt = bkt_ref[0]
    out = jnp.full(bkt.shape, NEG, F32)
    for b in range(N_BUCKETS):
        out = jnp.where(bkt == b, tab_ref[b, h] * LOG2E, out)
    o_ref[0, 0] = out


def _bias_call(rel_bias, buckets, head0, n_heads):
    nt, r, c = buckets.shape
    return pl.pallas_call(
        functools.partial(_bias_kernel, head0=head0),
        grid=(n_heads, nt),
        in_specs=[pl.BlockSpec(memory_space=pltpu.SMEM),
                  pl.BlockSpec((1, r, c), lambda h, t: (t, 0, 0))],
        out_specs=pl.BlockSpec((1, 1, r, c), lambda h, t: (h, t, 0, 0)),
        out_shape=jax.ShapeDtypeStruct((n_heads, nt, r, c), F32),
        name="bias_tiles",
    )(rel_bias, buckets)


def _tile_token_offsets():
    pos = jnp.arange(TILE, dtype=jnp.int32)
    c, a, il = pos // 64, (pos // GROUP_ROWS) % 4, pos % GROUP_ROWS
    return GROUP_ROWS * il + 4 * a + c


def _band_buckets(mq, mk, dil, first):
    m = mq[:, None] - mk[None, :]
    valid = (m >= 0) & (m <= BAND)
    if first:
        valid = valid & (mk[None, :] >= 0)
    return jnp.where(valid, _rel_bucket(m * dil), -1)


def _dilated_buckets():
    half = TILE // 2
    r = jnp.arange(half, dtype=jnp.int32)
    g, il_lo = r // 8, r % 8
    w = GROUP_ROWS * il_lo + 4 * (g % 4) + g // 4
    mk = jnp.concatenate([w - half, w])
    tiles_a = jnp.stack([_band_buckets(w, mk, 1, f) for f in (False, True)])
    r = jnp.arange(64, dtype=jnp.int32)
    j = 4 * (r % GROUP_ROWS) + r // GROUP_ROWS
    mq = jnp.concatenate([j, j + 64])
    mk = jnp.concatenate([j - 128, j - 64, j, j + 64])
    tiles_b = jnp.stack([_band_buckets(mq, mk, 4, f) for f in (False, True)])
    i = jnp.arange(TILE, dtype=jnp.int32)
    tiles_c = _band_buckets(i, i, 16, False)[None]
    return tiles_a, tiles_b, tiles_c


def _diff_buckets(seq, tile):
    u = _tile_token_offsets()
    i = u[None, None, :]
    j = u[None, :, None]
    d = jnp.arange(seq // tile, dtype=jnp.int32)[:, None, None]
    dist = d * tile + i - j
    return jnp.where(dist >= 0, _rel_bucket(dist), -1)


def _permute_kernel(x_ref, o_ref, *, inverse):
    for tt in range(PERM_ROWS // TILE):
        for c in range(4):
            for a in range(4):
                nat = pl.ds(tt * TILE + 4 * a + c, GROUP_ROWS, stride=GROUP_ROWS)
                stored = pl.ds(tt * TILE + 64 * c + GROUP_ROWS * a, GROUP_ROWS)
                if inverse:
                    o_ref[nat, :] = x_ref[stored, :]
                else:
                    o_ref[stored, :] = x_ref[nat, :]


def _permute_call(x2, inverse):
    m, d = x2.shape
    return pl.pallas_call(
        functools.partial(_permute_kernel, inverse=inverse),
        grid=(m // PERM_ROWS, d // LANES),
        in_specs=[pl.BlockSpec((PERM_ROWS, LANES), lambda i, j: (i, j))],
        out_specs=pl.BlockSpec((PERM_ROWS, LANES), lambda i, j: (i, j)),
        out_shape=jax.ShapeDtypeStruct((m, d), x2.dtype),
        compiler_params=pltpu.CompilerParams(
            dimension_semantics=("parallel", "parallel"), vmem_limit_bytes=VMEM_LIMIT),
        name="permute",
    )(x2)


def _inproj_kernel(x_ref, g_ref, w_ref, wvt_ref, cs_ref, o_ref, vt_ref):
    h = _rms(x_ref[...], g_ref[...], EPS).astype(BF16)
    o_ref[...] = (_dot(h, w_ref[...]) * cs_ref[...]).astype(BF16)
    vt = _nt_dot(wvt_ref[...], h).astype(BF16)
    for j in range(vt_ref.shape[0]):
        vt_ref[j] = vt[:, j * DIFF_TILE:(j + 1) * DIFF_TILE]


def _inproj_call(x2, g, w, wvt, colscale):
    m, d = x2.shape
    n, f = w.shape[1], wvt.shape[0]
    per_step = ROW_TILE // DIFF_TILE
    out_specs = [pl.BlockSpec((ROW_TILE, n), lambda i: (i, 0)),
                 pl.BlockSpec((per_step, f, DIFF_TILE), lambda i: (i, 0, 0))]
    out_shape = [jax.ShapeDtypeStruct((m, n), BF16),
                 jax.ShapeDtypeStruct((m // DIFF_TILE, f, DIFF_TILE), BF16)]
    return pl.pallas_call(
        _inproj_kernel,
        grid=(m // ROW_TILE,),
        in_specs=[pl.BlockSpec((ROW_TILE, d), lambda i: (i, 0)),
                  pl.BlockSpec((1, d), lambda i: (0, 0)),
                  pl.BlockSpec((d, n), lambda i: (0, 0)),
                  pl.BlockSpec((f, d), lambda i: (0, 0)),
                  pl.BlockSpec((1, n), lambda i: (0, 0))],
        out_specs=out_specs,
        out_shape=out_shape,
        compiler_params=pltpu.CompilerParams(
            dimension_semantics=("parallel",), vmem_limit_bytes=VMEM_LIMIT),
        name="inproj",
    )(x2, g, w, wvt, colscale)


def _head_masks(shape, dtype):
    lane = lax.broadcasted_iota(jnp.int32, shape, 1)
    return [jnp.where(sel, 1.0, 0.0).astype(dtype) for sel in (lane < HEAD_DIM, lane >= HEAD_DIM)]


def _v_ext(v):
    rows = [jnp.concatenate([v * mk, mk], axis=-1) for mk in _head_masks(v.shape, v.dtype)]
    return jnp.concatenate(rows, axis=0)


def _attend(q, k, v, bias, m_old, accl_old):
    r, kk = q.shape[0], k.shape[0]
    ps, m_new = [], []
    for h, mk in enumerate(_head_masks(q.shape, q.dtype)):
        s = _nt_dot(q * mk, k) + bias[h]
        mx = jnp.broadcast_to(jnp.max(s, axis=-1, keepdims=True), (r, LANES))
        mn = mx if m_old is None else jnp.maximum(m_old[h], mx)
        ps.append(jnp.exp2(s - jnp.concatenate([mn] * (kk // LANES), axis=-1)).astype(BF16))
        m_new.append(mn)
    accl = _dot(jnp.concatenate(ps, axis=-1), _v_ext(v))
    if m_old is not None:
        first_head = lax.broadcasted_iota(jnp.int32, (r, LANES), 1) < HEAD_DIM
        alpha = jnp.where(first_head, jnp.exp2(m_old[0] - m_new[0]), jnp.exp2(m_old[1] - m_new[1]))
        accl = jnp.concatenate([alpha, alpha], axis=-1) * accl_old + accl
    return m_new, accl


def _dil_kernel(q_ref, k_ref, v_ref, ba_ref, bb_ref, bc_ref, o_ref, m0_ref, m1_ref, accl_ref):
    n_tiles = q_ref.shape[1] // TILE
    half = TILE // 2
    m_refs = (m0_ref, m1_ref)

    def halves(ref, n):
        x = ref[0, pl.ds(pl.multiple_of(n * TILE, TILE), TILE), :].astype(F32)
        x = x.reshape(TILE // GROUP_ROWS, 2, 8, LANES)
        return [x[:, hh].reshape(half, LANES).astype(BF16) for hh in (0, 1)]

    def whole(x0, x1):
        w = x0.shape[-1]
        parts = [x.reshape(TILE // GROUP_ROWS, 1, 8, w) for x in (x0, x1)]
        return jnp.concatenate(parts, axis=1).reshape(TILE, w)

    def pass_a(n, carry):
        qh, kh, vh = halves(q_ref, n), halves(k_ref, n), halves(v_ref, n)
        prev = jnp.maximum(n - 1, 0)
        k_prev, v_prev = halves(k_ref, prev)[1], halves(v_ref, prev)[1]
        first = jnp.where(n == 0, 1, 0)
        res = []
        for hh in (0, 1):
            variant = first if hh == 0 else 0
            res.append(_attend(qh[hh],
                               jnp.concatenate([k_prev if hh == 0 else kh[0], kh[hh]], axis=0),
                               jnp.concatenate([v_prev if hh == 0 else vh[0], vh[hh]], axis=0),
                               [ba_ref[h, variant] for h in (0, 1)], None, None))
        rows = pl.ds(pl.multiple_of(n * TILE, TILE), TILE)
        accl_ref[rows, :] = whole(res[0][1], res[1][1])
        for h in (0, 1):
            m_refs[h][rows, :] = whole(res[0][0][h], res[1][0][h])
        return carry

    lax.fori_loop(0, n_tiles, pass_a, 0, unroll=True)

    def pass_b(nb, carry):
        t0 = 2 * nb
        first = jnp.where(nb == 0, 1, 0)
        key_tiles = [jnp.maximum(t0 - 2, 0), jnp.maximum(t0 - 1, 0), t0, t0 + 1]
        for c in range(4):
            def chunks(ref, tiles, batch=None):
                parts = []
                for t in tiles:
                    rows = pl.ds(pl.multiple_of(t * TILE + c * 64, 64), 64)
                    parts.append(ref[rows, :] if batch is None else ref[batch, rows, :])
                return jnp.concatenate(parts, axis=0)
            m_new, accl = _attend(chunks(q_ref, [t0, t0 + 1], 0), chunks(k_ref, key_tiles, 0),
                                  chunks(v_ref, key_tiles, 0),
                                  [bb_ref[h, first] for h in (0, 1)],
                                  [chunks(m_refs[h], [t0, t0 + 1]) for h in (0, 1)],
                                  chunks(accl_ref, [t0, t0 + 1]))
            for i, t in enumerate((t0, t0 + 1)):
                rows = pl.ds(pl.multiple_of(t * TILE + c * 64, 64), 64)
                accl_ref[rows, :] = accl[64 * i:64 * (i + 1)]
                for h in (0, 1):
                    m_refs[h][rows, :] = m_new[h][64 * i:64 * (i + 1)]
        return carry

    lax.fori_loop(0, n_tiles // 2, pass_b, 0, unroll=True)

    def pass_c(a, carry):
        for c in range(4):
            def atoms(ref, batch=None):
                parts = []
                for t in range(n_tiles):
                    rows = pl.ds(pl.multiple_of(t * TILE + c * 64 + a * GROUP_ROWS, GROUP_ROWS),
                                 GROUP_ROWS)
                    parts.append(ref[rows, :] if batch is None else ref[batch, rows, :])
                return jnp.concatenate(parts, axis=0)
            _, accl = _attend(atoms(q_ref, 0), atoms(k_ref, 0), atoms(v_ref, 0),
                              [bc_ref[h, 0] for h in (0, 1)],
                              [atoms(m_refs[h]) for h in (0, 1)], atoms(accl_ref))
            out = (accl[:, :LANES] / accl[:, LANES:]).astype(BF16)
            for t in range(n_tiles):
                rows = pl.ds(pl.multiple_of(t * TILE + c * 64 + a * GROUP_ROWS, GROUP_ROWS),
                             GROUP_ROWS)
                o_ref[0, rows, :] = out[GROUP_ROWS * t:GROUP_ROWS * (t + 1)]
        return carry

    lax.fori_loop(0, 4, pass_c, 0, unroll=True)


def _dil_call(proj, bias_a, bias_b, bias_c):
    b, s, w = proj.shape
    assert s == TILE * GROUP_ROWS
    hw = 2 * HEAD_DIM
    n_pairs = N_HEADS_DIL // 2
    k0, v0 = DIL_WIDTH // hw, 2 * DIL_WIDTH // hw

    def bias_spec(bias):
        return pl.BlockSpec((2,) + bias.shape[1:], lambda hp, bi: (hp, 0, 0, 0))

    return pl.pallas_call(
        _dil_kernel,
        grid=(n_pairs, b),
        in_specs=[pl.BlockSpec((1, s, hw), lambda hp, bi: (bi, 0, hp)),
                  pl.BlockSpec((1, s, hw), lambda hp, bi: (bi, 0, k0 + hp)),
                  pl.BlockSpec((1, s, hw), lambda hp, bi: (bi, 0, v0 + hp)),
                  bias_spec(bias_a), bias_spec(bias_b), bias_spec(bias_c)],
        out_specs=pl.BlockSpec((1, s, hw), lambda hp, bi: (bi, 0, hp)),
        out_shape=jax.ShapeDtypeStruct((b, s, DIL_WIDTH), BF16),
        scratch_shapes=[pltpu.VMEM((s, LANES), F32),
                        pltpu.VMEM((s, LANES), F32),
                        pltpu.VMEM((s, 2 * LANES), F32)],
        compiler_params=pltpu.CompilerParams(
            dimension_semantics=("parallel", "parallel"), vmem_limit_bytes=VMEM_LIMIT),
        name="dilated_attn",
    )(proj, proj, proj, bias_a, bias_b, bias_c)


def _diff_kernel(lq_ref, q_ref, k_ref, vt_ref, bias_ref, g_ref, o_ref,
                 acc_ref, sa_ref, sb_ref, qq_ref, *, lam_init):
    t = DIFF_TILE
    hw = 2 * HEAD_DIM
    ones = jnp.ones((ONES_ROWS, t), BF16)
    lq = lq_ref[0]
    lam = (jnp.exp(jnp.sum(lq[0:1] * lq[1:2], axis=-1, keepdims=True))
           - jnp.exp(jnp.sum(lq[2:3] * lq[3:4], axis=-1, keepdims=True)) + lam_init)

    n_tiles = q_ref.shape[1] // t
    streams = range(q_ref.shape[0])

    def load_queries(qi):
        rows = pl.ds(qi * t, t)
        bias0 = bias_ref[0, qi]
        bias0 = jnp.concatenate([bias0, bias0], axis=-1)
        for e in streams:
            q = q_ref[e, rows, :]
            lane = lax.broadcasted_iota(jnp.int32, q.shape, 1)
            zero = jnp.zeros_like(q)
            qq = jnp.concatenate([jnp.where(lane < HEAD_DIM, q, zero),
                                  jnp.where(lane >= HEAD_DIM, q, zero)], axis=0)
            qq_ref[e] = qq
            sa_ref[e] = _nt_dot(k_ref[e, pl.ds(0, t), :], qq) + bias0

    def q_block(qi):
        def scores(e, kj):
            off = pl.multiple_of(kj * t, t)
            bias = bias_ref[0, qi - kj]
            return (_nt_dot(k_ref[e, pl.ds(off, t), :], qq_ref[e])
                    + jnp.concatenate([bias, bias], axis=-1))

        def accumulate(e, s, kj, m_prev):
            vt_ext = jnp.concatenate([vt_ref[e * n_tiles + kj], ones], axis=0)
            m_next = jnp.maximum(m_prev, jnp.max(s, axis=0, keepdims=True))
            p = jnp.exp2(s - m_next).astype(BF16)
            acc_ref[e] = jnp.exp2(m_prev - m_next) * acc_ref[e] + _dot(vt_ext, p)
            return m_next

        def pair(k0, ms, prefetch):
            for e in streams:
                sb_ref[e] = scores(e, k0 + 1)
            ms = [accumulate(e, sa_ref[e], k0, ms[e]) for e in streams]
            if prefetch:
                for e in streams:
                    sa_ref[e] = scores(e, k0 + 2)
            return [accumulate(e, sb_ref[e], k0 + 1, ms[e]) for e in streams]

        n_blocks = qi + 1
        odd = n_blocks % 2
        full = n_blocks // 2 if odd else n_blocks // 2 - 1
        ms = [jnp.full((1, 2 * t), NEG, F32) for _ in streams]
        if full:
            ms = lax.fori_loop(0, full, lambda jj, ms: pair(2 * jj, ms, True), ms)
        if odd:
            ms = [accumulate(e, sa_ref[e], qi, ms[e]) for e in streams]
        else:
            ms = pair(qi - 1, ms, False)

        rows = pl.ds(qi * t, t)
        for e in streams:
            acc = acc_ref[e]
            o = acc[:hw] / acc[hw:hw + 1]
            a = (o[:, :t] - lam * o[:, t:]).T
            o_ref[e, rows, :] = (_rms(a, g_ref[0], SUBLN_EPS) * (1.0 - lam_init)).astype(BF16)

    acc_ref[...] = jnp.zeros(acc_ref.shape, F32)
    for qi in range(n_tiles):
        load_queries(qi)
        q_block(qi)


def _diff_call(proj, vt, bias, lq, g, *, lam_init):
    b, s, w = proj.shape
    t = DIFF_TILE
    hw = 2 * HEAD_DIM
    q0 = 3 * DIL_WIDTH // hw
    k0 = q0 + N_HEADS_DIFF
    ns = DIFF_STREAMS
    assert b % ns == 0
    return pl.pallas_call(
        functools.partial(_diff_kernel, lam_init=lam_init),
        grid=(N_HEADS_DIFF, b // ns),
        in_specs=[pl.BlockSpec((1, 4, HEAD_DIM), lambda h, bi: (0, 0, 0)),
                  pl.BlockSpec((ns, s, hw), lambda h, bi: (bi, 0, q0 + h)),
                  pl.BlockSpec((ns, s, hw), lambda h, bi: (bi, 0, k0 + h)),
                  pl.BlockSpec((ns * (s // t), hw, t), lambda h, bi: (bi, h, 0)),
                  pl.BlockSpec((1, s // t, t, t), lambda h, bi: (h, 0, 0, 0)),
                  pl.BlockSpec((1, 1, hw), lambda h, bi: (0, 0, 0))],
        out_specs=pl.BlockSpec((ns, s, hw), lambda h, bi: (bi, 0, h)),
        out_shape=jax.ShapeDtypeStruct((b, s, DIFF_WIDTH), BF16),
        scratch_shapes=[pltpu.VMEM((ns, hw + ONES_ROWS, 2 * t), F32),
                        pltpu.VMEM((ns, t, 2 * t), F32),
                        pltpu.VMEM((ns, t, 2 * t), F32),
                        pltpu.VMEM((ns, 2 * t, hw), BF16)],
        compiler_params=pltpu.CompilerParams(
            dimension_semantics=("parallel", "parallel"), vmem_limit_bytes=VMEM_LIMIT),
        name="diff_attn",
    )(lq, proj, proj, vt, bias, g)


def _ff_chunks(d_ff):
    chunks, c0 = [], 0
    while c0 < d_ff:
        cw = min(FF_CHUNK, d_ff - c0)
        chunks.append((c0, cw))
        c0 += cw
    return chunks


def _mlp_kernel(*refs, final):
    x_ref, oa_ref, ob_ref, wo_ref, g_ref, wgu_ref, wd_ref = refs[:7]
    o_ref, act_ref = refs[-2:]
    d_ff = wd_ref.shape[0]
    x = (x_ref[...] + _dot(oa_ref[...], wo_ref[:DIL_WIDTH, :])
         + _dot(ob_ref[...], wo_ref[DIL_WIDTH:, :]))
    h = _rms(x, g_ref[...], EPS).astype(BF16)
    for c0, cw in _ff_chunks(d_ff):
        gate = _dot(h, wgu_ref[:, c0:c0 + cw])
        up = _dot(h, wgu_ref[:, d_ff + c0:d_ff + c0 + cw])
        act_ref[:, c0:c0 + cw] = ((gate * jax.nn.sigmoid(gate)) * up).astype(BF16)
    x = x + _dot(act_ref[...], wd_ref[...])
    if final:
        x = _rms(x, refs[7][...], EPS)
    o_ref[...] = x


def _mlp_call(x2, oa, ob, wo, g, wgu, wd, g_final):
    m, d = x2.shape
    d_ff = wd.shape[0]
    final = g_final is not None
    row = lambda i: (i, 0)
    fixed = lambda i: (0, 0)
    in_specs = [pl.BlockSpec((ROW_TILE, d), row),
                pl.BlockSpec((ROW_TILE, DIL_WIDTH), row),
                pl.BlockSpec((ROW_TILE, DIFF_WIDTH), row),
                pl.BlockSpec((MIX_WIDTH, d), fixed, pipeline_mode=pl.Buffered(1)),
                pl.BlockSpec((1, d), fixed),
                pl.BlockSpec((d, 2 * d_ff), fixed, pipeline_mode=pl.Buffered(1)),
                pl.BlockSpec((d_ff, d), fixed, pipeline_mode=pl.Buffered(1))]
    args = [x2, oa, ob, wo, g, wgu, wd]
    if final:
        in_specs.append(pl.BlockSpec((1, d), fixed))
        args.append(g_final)
    return pl.pallas_call(
        functools.partial(_mlp_kernel, final=final),
        grid=(m // ROW_TILE,),
        in_specs=in_specs,
        out_specs=pl.BlockSpec((ROW_TILE, d), row),
        out_shape=jax.ShapeDtypeStruct((m, d), F32),
        scratch_shapes=[pltpu.VMEM((ROW_TILE, d_ff), BF16)],
        compiler_params=pltpu.CompilerParams(
            dimension_semantics=("parallel",), vmem_limit_bytes=VMEM_LIMIT),
        name="outproj_mlp",
    )(*args)


def kernel(x, g_attn, w_in, w_out, rel_bias, lambda_qk, subln_g, g_ffn,
           w_gate_up, w_down, g_final):
    b, s, d = x.shape
    depth = w_in.shape[0]
    assert (b * s) % PERM_ROWS == 0 and PERM_ROWS % ROW_TILE == 0 and d % LANES == 0

    bias_a, bias_b, bias_c = (_bias_call(rel_bias, bk, 0, N_HEADS_DIL) for bk in _dilated_buckets())
    bias_diff = _bias_call(rel_bias, _diff_buckets(s, DIFF_TILE), N_HEADS_DIL, N_HEADS_DIFF)

    col = jnp.arange(w_in.shape[-1] - DIFF_WIDTH) // DIL_WIDTH
    qscale = jnp.where((col == 0) | (col == 3), LOG2E * HEAD_DIM ** -0.5, 1.0).astype(F32)[None]

    x2 = _permute_call(x.reshape(b * s, d), inverse=False)
    for l in range(depth):
        lam_init = 0.8 - 0.6 * math.exp(-0.3 * l)
        w_in_l = w_in[l].astype(BF16)
        n_tok = w_in_l.shape[1] - DIFF_WIDTH
        proj, vt = _inproj_call(x2, g_attn[l][None], w_in_l[:, :n_tok], w_in_l[:, n_tok:].T, qscale)
        proj = proj.reshape(b, s, -1)
        oa = _dil_call(proj, bias_a, bias_b, bias_c)
        ob = _diff_call(proj, vt, bias_diff, lambda_qk[l][None], subln_g[l][None, None],
                        lam_init=lam_init)
        x2 = _mlp_call(x2, oa.reshape(b * s, DIL_WIDTH), ob.reshape(b * s, DIFF_WIDTH),
                       w_out[l].astype(BF16), g_ffn[l][None],
                       w_gate_up[l].astype(BF16), w_down[l].astype(BF16),
                       g_final[None] if l == depth - 1 else None)
    return _permute_call(x2, inverse=True).reshape(b, s, d)
```

```python
import functools
import math

import jax
import jax.numpy as jnp
from jax import lax
from jax.experimental import pallas as pl
from jax.experimental.pallas import tpu as pltpu

F32 = jnp.float32
BF16 = jnp.bfloat16

HEAD_DIM = 64
N_HEADS_DIL = 8
N_HEADS_DIFF = 4
DIL_WIDTH = N_HEADS_DIL * HEAD_DIM
DIFF_WIDTH = N_HEADS_DIFF * 2 * HEAD_DIM
MIX_WIDTH = DIL_WIDTH + DIFF_WIDTH
BAND = 128
N_BUCKETS = 32
MAX_DISTANCE = 2048
EPS = 1e-6
SUBLN_EPS = 1e-5
NEG = -1e30
LOG2E = math.log2(math.e)

LANES = 128
VMEM_LIMIT = 56 * 1024 * 1024
ROW_TILE = 1024
TILE = 256
GROUP_ROWS = 16
PERM_ROWS = 8192
DIFF_TILE = TILE
FF_CHUNK = 768
ONES_ROWS = 16
DIFF_STREAMS = 2


def _nt_dot(a, b):
    return lax.dot_general(a, b, (((1,), (1,)), ((), ())), preferred_element_type=F32)


def _dot(a, b):
    return jnp.dot(a, b, preferred_element_type=F32)


def _rms(x, g, eps):
    return (x * lax.rsqrt(jnp.mean(x * x, axis=-1, keepdims=True) + eps)) * g


def _rel_bucket(dist):
    max_exact = N_BUCKETS // 2
    n = jnp.maximum(dist, 0)
    nf = jnp.maximum(n, 1).astype(F32)
    large = max_exact + (jnp.log(nf / max_exact) / math.log(MAX_DISTANCE / max_exact)
                         * (N_BUCKETS - max_exact)).astype(jnp.int32)
    large = jnp.minimum(large, N_BUCKETS - 1)
    return jnp.where(n < max_exact, n, large)


def _bias_kernel(tab_ref, bkt_ref, o_ref, *, head0):
    h = pl.program_id(0) + head0
    bkt = bkt_ref[0]
    out = jnp.full(bkt.shape, NEG, F32)
    for b in range(N_BUCKETS):
        out = jnp.where(bkt == b, tab_ref[b, h] * LOG2E, out)
    o_ref[0, 0] = out


def _bias_call(rel_bias, buckets, head0, n_heads):
    nt, r, c = buckets.shape
    return pl.pallas_call(
        functools.partial(_bias_kernel, head0=head0),
        grid=(n_heads, nt),
        in_specs=[pl.BlockSpec(memory_space=pltpu.SMEM),
                  pl.BlockSpec((1, r, c), lambda h, t: (t, 0, 0))],
        out_specs=pl.BlockSpec((1, 1, r, c), lambda h, t: (h, t, 0, 0)),
        out_shape=jax.ShapeDtypeStruct((n_heads, nt, r, c), F32),
        name="bias_tiles",
    )(rel_bias, buckets)


def _tile_token_offsets():
    pos = jnp.arange(TILE, dtype=jnp.int32)
    c, a, il = pos // 64, (pos // GROUP_ROWS) % 4, pos % GROUP_ROWS
    return GROUP_ROWS * il + 4 * a + c


def _band_buckets(mq, mk, dil, first):
    m = mq[:, None] - mk[None, :]
    valid = (m >= 0) & (m <= BAND)
    if first:
        valid = valid & (mk[None, :] >= 0)
    return jnp.where(valid, _rel_bucket(m * dil), -1)


def _dilated_buckets():
    half = TILE // 2
    r = jnp.arange(half, dtype=jnp.int32)
    g, il_lo = r // 8, r % 8
    w = GROUP_ROWS * il_lo + 4 * (g % 4) + g // 4
    mk = jnp.concatenate([w - half, w])
    tiles_a = jnp.stack([_band_buckets(w, mk, 1, f) for f in (False, True)])
    r = jnp.arange(64, dtype=jnp.int32)
    j = 4 * (r % GROUP_ROWS) + r // GROUP_ROWS
    mq = jnp.concatenate([j, j + 64])
    mk = jnp.concatenate([j - 128, j - 64, j, j + 64])
    tiles_b = jnp.stack([_band_buckets(mq, mk, 4, f) for f in (False, True)])
    i = jnp.arange(TILE, dtype=jnp.int32)
    tiles_c = _band_buckets(i, i, 16, False)[None]
    return tiles_a, tiles_b, tiles_c


def _diff_buckets(seq, tile):
    u = _tile_token_offsets()
    i = u[None, None, :]
    j = u[None, :, None]
    d = jnp.arange(seq // tile, dtype=jnp.int32)[:, None, None]
    dist = d * tile + i - j
    return jnp.where(dist >= 0, _rel_bucket(dist), -1)


def _permute_kernel(x_ref, o_ref, *, inverse):
    for tt in range(PERM_ROWS // TILE):
        for c in range(4):
            for a in range(4):
                nat = pl.ds(tt * TILE + 4 * a + c, GROUP_ROWS, stride=GROUP_ROWS)
                stored = pl.ds(tt * TILE + 64 * c + GROUP_ROWS * a, GROUP_ROWS)
                if inverse:
                    o_ref[nat, :] = x_ref[stored, :]
                else:
                    o_ref[stored, :] = x_ref[nat, :]


def _permute_call(x2, inverse):
    m, d = x2.shape
    return pl.pallas_call(
        functools.partial(_permute_kernel, inverse=inverse),
        grid=(m // PERM_ROWS, d // LANES),
        in_specs=[pl.BlockSpec((PERM_ROWS, LANES), lambda i, j: (i, j))],
        out_specs=pl.BlockSpec((PERM_ROWS, LANES), lambda i, j: (i, j)),
        out_shape=jax.ShapeDtypeStruct((m, d), x2.dtype),
        compiler_params=pltpu.CompilerParams(
            dimension_semantics=("parallel", "parallel"), vmem_limit_bytes=VMEM_LIMIT),
        name="permute",
    )(x2)


def _inproj_kernel(x_ref, g_ref, w_ref, wvt_ref, cs_ref, o_ref, vt_ref):
    h = _rms(x_ref[...], g_ref[...], EPS).astype(BF16)
    o_ref[...] = (_dot(h, w_ref[...]) * cs_ref[...]).astype(BF16)
    vt = _nt_dot(wvt_ref[...], h).astype(BF16)
    for j in range(vt_ref.shape[0]):
        vt_ref[j] = vt[:, j * DIFF_TILE:(j + 1) * DIFF_TILE]


def _inproj_call(x2, g, w, wvt, colscale):
    m, d = x2.shape
    n, f = w.shape[1], wvt.shape[0]
    per_step = ROW_TILE // DIFF_TILE
    out_specs = [pl.BlockSpec((ROW_TILE, n), lambda i: (i, 0)),
                 pl.BlockSpec((per_step, f, DIFF_TILE), lambda i: (i, 0, 0))]
    out_shape = [jax.ShapeDtypeStruct((m, n), BF16),
                 jax.ShapeDtypeStruct((m // DIFF_TILE, f, DIFF_TILE), BF16)]
    return pl.pallas_call(
        _inproj_kernel,
        grid=(m // ROW_TILE,),
        in_specs=[pl.BlockSpec((ROW_TILE, d), lambda i: (i, 0)),
                  pl.BlockSpec((1, d), lambda i: (0, 0)),
                  pl.BlockSpec((d, n), lambda i: (0, 0)),
                  pl.BlockSpec((f, d), lambda i: (0, 0)),
                  pl.BlockSpec((1, n), lambda i: (0, 0))],
        out_specs=out_specs,
        out_shape=out_shape,
        compiler_params=pltpu.CompilerParams(
            dimension_semantics=("parallel",), vmem_limit_bytes=VMEM_LIMIT),
        name="inproj",
    )(x2, g, w, wvt, colscale)


def _head_masks(shape, dtype):
    lane = lax.broadcasted_iota(jnp.int32, shape, 1)
    return [jnp.where(sel, 1.0, 0.0).astype(dtype) for sel in (lane < HEAD_DIM, lane >= HEAD_DIM)]


def _v_ext(v):
    rows = [jnp.concatenate([v * mk, mk], axis=-1) for mk in _head_masks(v.shape, v.dtype)]
    return jnp.concatenate(rows, axis=0)


def _attend(q, k, v, bias, m_old, accl_old):
    r, kk = q.shape[0], k.shape[0]
    ps, m_new = [], []
    for h, mk in enumerate(_head_masks(q.shape, q.dtype)):
        s = _nt_dot(q * mk, k) + bias[h]
        mx = jnp.broadcast_to(jnp.max(s, axis=-1, keepdims=True), (r, LANES))
        mn = mx if m_old is None else jnp.maximum(m_old[h], mx)
        ps.append(jnp.exp2(s - jnp.concatenate([mn] * (kk // LANES), axis=-1)).astype(BF16))
        m_new.append(mn)
    accl = _dot(jnp.concatenate(ps, axis=-1), _v_ext(v))
    if m_old is not None:
        first_head = lax.broadcasted_iota(jnp.int32, (r, LANES), 1) < HEAD_DIM
        alpha = jnp.where(first_head, jnp.exp2(m_old[0] - m_new[0]), jnp.exp2(m_old[1] - m_new[1]))
        accl = jnp.concatenate([alpha, alpha], axis=-1) * accl_old + accl
    return m_new, accl


def _dil_kernel(q_ref, k_ref, v_ref, ba_ref, bb_ref, bc_ref, o_ref, m0_ref, m1_ref, accl_ref):
    n_tiles = q_ref.shape[1] // TILE
    half = TILE // 2
    m_refs = (m0_ref, m1_ref)

    def halves(ref, n):
        x = ref[0, pl.ds(pl.multiple_of(n * TILE, TILE), TILE), :].astype(F32)
        x = x.reshape(TILE // GROUP_ROWS, 2, 8, LANES)
        return [x[:, hh].reshape(half, LANES).astype(BF16) for hh in (0, 1)]

    def whole(x0, x1):
        w = x0.shape[-1]
        parts = [x.reshape(TILE // GROUP_ROWS, 1, 8, w) for x in (x0, x1)]
        return jnp.concatenate(parts, axis=1).reshape(TILE, w)

    def pass_a(n, carry):
        qh, kh, vh = halves(q_ref, n), halves(k_ref, n), halves(v_ref, n)
        prev = jnp.maximum(n - 1, 0)
        k_prev, v_prev = halves(k_ref, prev)[1], halves(v_ref, prev)[1]
        first = jnp.where(n == 0, 1, 0)
        res = []
        for hh in (0, 1):
            variant = first if hh == 0 else 0
            res.append(_attend(qh[hh],
                               jnp.concatenate([k_prev if hh == 0 else kh[0], kh[hh]], axis=0),
                               jnp.concatenate([v_prev if hh == 0 else vh[0], vh[hh]], axis=0),
                               [ba_ref[h, variant] for h in (0, 1)], None, None))
        rows = pl.ds(pl.multiple_of(n * TILE, TILE), TILE)
        accl_ref[rows, :] = whole(res[0][1], res[1][1])
        for h in (0, 1):
            m_refs[h][rows, :] = whole(res[0][0][h], res[1][0][h])
        return carry

    lax.fori_loop(0, n_tiles, pass_a, 0, unroll=True)

    def pass_b(nb, carry):
        t0 = 2 * nb
        first = jnp.where(nb == 0, 1, 0)
        key_tiles = [jnp.maximum(t0 - 2, 0), jnp.maximum(t0 - 1, 0), t0, t0 + 1]
        for c in range(4):
            def chunks(ref, tiles, batch=None):
                parts = []
                for t in tiles:
                    rows = pl.ds(pl.multiple_of(t * TILE + c * 64, 64), 64)
                    parts.append(ref[rows, :] if batch is None else ref[batch, rows, :])
                return jnp.concatenate(parts, axis=0)
            m_new, accl = _attend(chunks(q_ref, [t0, t0 + 1], 0), chunks(k_ref, key_tiles, 0),
                                  chunks(v_ref, key_tiles, 0),
                                  [bb_ref[h, first] for h in (0, 1)],
                                  [chunks(m_refs[h], [t0, t0 + 1]) for h in (0, 1)],
                                  chunks(accl_ref, [t0, t0 + 1]))
            for i, t in enumerate((t0, t0 + 1)):
                rows = pl.ds(pl.multiple_of(t * TILE + c * 64, 64), 64)
                accl_ref[rows, :] = accl[64 * i:64 * (i + 1)]
                for h in (0, 1):
                    m_refs[h][rows, :] = m_new[h][64 * i:64 * (i + 1)]
        return carry

    lax.fori_loop(0, n_tiles // 2, pass_b, 0, unroll=True)

    def pass_c(a, carry):
        for c in range(4):
            def atoms(ref, batch=None):
                parts = []
                for t in range(n_tiles):
                    rows = pl.ds(pl.multiple_of(t * TILE + c * 64 + a * GROUP_ROWS, GROUP_ROWS),
                                 GROUP_ROWS)
                    parts.append(ref[rows, :] if batch is None else ref[batch, rows, :])
                return jnp.concatenate(parts, axis=0)
            _, accl = _attend(atoms(q_ref, 0), atoms(k_ref, 0), atoms(v_ref, 0),
                              [bc_ref[h, 0] for h in (0, 1)],
                              [atoms(m_refs[h]) for h in (0, 1)], atoms(accl_ref))
            out = (accl[:, :LANES] / accl[:, LANES:]).astype(BF16)
            for t in range(n_tiles):
                rows = pl.ds(pl.multiple_of(t * TILE + c * 64 + a * GROUP_ROWS, GROUP_ROWS),
                             GROUP_ROWS)
                o_ref[0, rows, :] = out[GROUP_ROWS * t:GROUP_ROWS * (t + 1)]
        return carry

    lax.fori_loop(0, 4, pass_c, 0, unroll=True)


def _dil_call(proj, bias_a, bias_b, bias_c):
    b, s, w = proj.shape
    assert s == TILE * GROUP_ROWS
    hw = 2 * HEAD_DIM
    n_pairs = N_HEADS_DIL // 2
    k0, v0 = DIL_WIDTH // hw, 2 * DIL_WIDTH // hw

    def bias_spec(bias):
        return pl.BlockSpec((2,) + bias.shape[1:], lambda hp, bi: (hp, 0, 0, 0))

    return pl.pallas_call(
        _dil_kernel,
        grid=(n_pairs, b),
        in_specs=[pl.BlockSpec((1, s, hw), lambda hp, bi: (bi, 0, hp)),
                  pl.BlockSpec((1, s, hw), lambda hp, bi: (bi, 0, k0 + hp)),
                  pl.BlockSpec((1, s, hw), lambda hp, bi: (bi, 0, v0 + hp)),
                  bias_spec(bias_a), bias_spec(bias_b), bias_spec(bias_c)],
        out_specs=pl.BlockSpec((1, s, hw), lambda hp, bi: (bi, 0, hp)),
        out_shape=jax.ShapeDtypeStruct((b, s, DIL_WIDTH), BF16),
        scratch_shapes=[pltpu.VMEM((s, LANES), F32),
                        pltpu.VMEM((s, LANES), F32),
                        pltpu.VMEM((s, 2 * LANES), F32)],
        compiler_params=pltpu.CompilerParams(
            dimension_semantics=("parallel", "parallel"), vmem_limit_bytes=VMEM_LIMIT),
        name="dilated_attn",
    )(proj, proj, proj, bias_a, bias_b, bias_c)


def _diff_kernel(lq_ref, q_ref, k_ref, vt_ref, bias_ref, g_ref, o_ref,
                 acc_ref, sa_ref, sb_ref, qq_ref, *, lam_init):
    t = DIFF_TILE
    hw = 2 * HEAD_DIM
    ones = jnp.ones((ONES_ROWS, t), BF16)
    lq = lq_ref[0]
    lam = (jnp.exp(jnp.sum(lq[0:1] * lq[1:2], axis=-1, keepdims=True))
           - jnp.exp(jnp.sum(lq[2:3] * lq[3:4], axis=-1, keepdims=True)) + lam_init)

    n_tiles = q_ref.shape[1] // t
    streams = range(q_ref.shape[0])

    def load_queries(qi):
        rows = pl.ds(qi * t, t)
        bias0 = bias_ref[0, qi]
        bias0 = jnp.concatenate([bias0, bias0], axis=-1)
        for e in streams:
            q = q_ref[e, rows, :]
            lane = lax.broadcasted_iota(jnp.int32, q.shape, 1)
            zero = jnp.zeros_like(q)
            qq = jnp.concatenate([jnp.where(lane < HEAD_DIM, q, zero),
                                  jnp.where(lane >= HEAD_DIM, q, zero)], axis=0)
            qq_ref[e] = qq
            sa_ref[e] = _nt_dot(k_ref[e, pl.ds(0, t), :], qq) + bias0

    def q_block(qi):
        def scores(e, kj):
            off = pl.multiple_of(kj * t, t)
            bias = bias_ref[0, qi - kj]
            return (_nt_dot(k_ref[e, pl.ds(off, t), :], qq_ref[e])
                    + jnp.concatenate([bias, bias], axis=-1))

        def accumulate(e, s, kj, m_prev):
            vt_ext = jnp.concatenate([vt_ref[e * n_tiles + kj], ones], axis=0)
            m_next = jnp.maximum(m_prev, jnp.max(s, axis=0, keepdims=True))
            p = jnp.exp2(s - m_next).astype(BF16)
            acc_ref[e] = jnp.exp2(m_prev - m_next) * acc_ref[e] + _dot(vt_ext, p)
            return m_next

        def pair(k0, ms, prefetch):
            for e in streams:
                sb_ref[e] = scores(e, k0 + 1)
            ms = [accumulate(e, sa_ref[e], k0, ms[e]) for e in streams]
            if prefetch:
                for e in streams:
                    sa_ref[e] = scores(e, k0 + 2)
            return [accumulate(e, sb_ref[e], k0 + 1, ms[e]) for e in streams]

        n_blocks = qi + 1
        odd = n_blocks % 2
        full = n_blocks // 2 if odd else n_blocks // 2 - 1
        ms = [jnp.full((1, 2 * t), NEG, F32) for _ in streams]
        if full:
            ms = lax.fori_loop(0, full, lambda jj, ms: pair(2 * jj, ms, True), ms, unroll=2)
        if odd:
            ms = [accumulate(e, sa_ref[e], qi, ms[e]) for e in streams]
        else:
            ms = pair(qi - 1, ms, False)

        rows = pl.ds(qi * t, t)
        for e in streams:
            acc = acc_ref[e]
            o = acc[:hw] / acc[hw:hw + 1]
            a = (o[:, :t] - lam * o[:, t:]).T
            o_ref[e, rows, :] = (_rms(a, g_ref[0], SUBLN_EPS) * (1.0 - lam_init)).astype(BF16)

    acc_ref[...] = jnp.zeros(acc_ref.shape, F32)
    for qi in range(n_tiles):
        load_queries(qi)
        q_block(qi)


def _diff_call(proj, vt, bias, lq, g, *, lam_init):
    b, s, w = proj.shape
    t = DIFF_TILE
    hw = 2 * HEAD_DIM
    q0 = 3 * DIL_WIDTH // hw
    k0 = q0 + N_HEADS_DIFF
    ns = DIFF_STREAMS
    assert b % ns == 0
    return pl.pallas_call(
        functools.partial(_diff_kernel, lam_init=lam_init),
        grid=(N_HEADS_DIFF, b // ns),
        in_specs=[pl.BlockSpec((1, 4, HEAD_DIM), lambda h, bi: (0, 0, 0)),
                  pl.BlockSpec((ns, s, hw), lambda h, bi: (bi, 0, q0 + h)),
                  pl.BlockSpec((ns, s, hw), lambda h, bi: (bi, 0, k0 + h)),
                  pl.BlockSpec((ns * (s // t), hw, t), lambda h, bi: (bi, h, 0)),
                  pl.BlockSpec((1, s // t, t, t), lambda h, bi: (h, 0, 0, 0)),
                  pl.BlockSpec((1, 1, hw), lambda h, bi: (0, 0, 0))],
        out_specs=pl.BlockSpec((ns, s, hw), lambda h, bi: (bi, 0, h)),
        out_shape=jax.ShapeDtypeStruct((b, s, DIFF_WIDTH), BF16),
        scratch_shapes=[pltpu.VMEM((ns, hw + ONES_ROWS, 2 * t), F32),
                        pltpu.VMEM((ns, t, 2 * t), F32),
                        pltpu.VMEM((ns, t, 2 * t), F32),
                        pltpu.VMEM((ns, 2 * t, hw), BF16)],
        compiler_params=pltpu.CompilerParams(
            dimension_semantics=("parallel", "parallel"), vmem_limit_bytes=VMEM_LIMIT),
        name="diff_attn",
    )(lq, proj, proj, vt, bias, g)


def _ff_chunks(d_ff):
    chunks, c0 = [], 0
    while c0 < d_ff:
        cw = min(FF_CHUNK, d_ff - c0)
        chunks.append((c0, cw))
        c0 += cw
    return chunks


def _mlp_kernel(*refs, final):
    x_ref, oa_ref, ob_ref, wo_ref, g_ref, wgu_ref, wd_ref = refs[:7]
    o_ref, act_ref = refs[-2:]
    d_ff = wd_ref.shape[0]
    x = (x_ref[...] + _dot(oa_ref[...], wo_ref[:DIL_WIDTH, :])
         + _dot(ob_ref[...], wo_ref[DIL_WIDTH:, :]))
    h = _rms(x, g_ref[...], EPS).astype(BF16)
    for c0, cw in _ff_chunks(d_ff):
        gate = _dot(h, wgu_ref[:, c0:c0 + cw])
        up = _dot(h, wgu_ref[:, d_ff + c0:d_ff + c0 + cw])
        act_ref[:, c0:c0 + cw] = ((gate * jax.nn.sigmoid(gate)) * up).astype(BF16)
    x = x + _dot(act_ref[...], wd_ref[...])
    if final:
        x = _rms(x, refs[7][...], EPS)
    o_ref[...] = x


def _mlp_call(x2, oa, ob, wo, g, wgu, wd, g_final):
    m, d = x2.shape
    d_ff = wd.shape[0]
    final = g_final is not None
    row = lambda i: (i, 0)
    fixed = lambda i: (0, 0)
    in_specs = [pl.BlockSpec((ROW_TILE, d), row),
                pl.BlockSpec((ROW_TILE, DIL_WIDTH), row),
                pl.BlockSpec((ROW_TILE, DIFF_WIDTH), row),
                pl.BlockSpec((MIX_WIDTH, d), fixed, pipeline_mode=pl.Buffered(1)),
                pl.BlockSpec((1, d), fixed),
                pl.BlockSpec((d, 2 * d_ff), fixed, pipeline_mode=pl.Buffered(1)),
                pl.BlockSpec((d_ff, d), fixed, pipeline_mode=pl.Buffered(1))]
    args = [x2, oa, ob, wo, g, wgu, wd]
    if final:
        in_specs.append(pl.BlockSpec((1, d), fixed))
        args.append(g_final)
    return pl.pallas_call(
        functools.partial(_mlp_kernel, final=final),
        grid=(m // ROW_TILE,),
        in_specs=in_specs,
        out_specs=pl.BlockSpec((ROW_TILE, d), row),
        out_shape=jax.ShapeDtypeStruct((m, d), F32),
        scratch_shapes=[pltpu.VMEM((ROW_TILE, d_ff), BF16)],
        compiler_params=pltpu.CompilerParams(
            dimension_semantics=("parallel",), vmem_limit_bytes=VMEM_LIMIT),
        name="outproj_mlp",
    )(*args)


def kernel(x, g_attn, w_in, w_out, rel_bias, lambda_qk, subln_g, g_ffn,
           w_gate_up, w_down, g_final):
    b, s, d = x.shape
    depth = w_in.shape[0]
    assert (b * s) % PERM_ROWS == 0 and PERM_ROWS % ROW_TILE == 0 and d % LANES == 0

    bias_a, bias_b, bias_c = (_bias_call(rel_bias, bk, 0, N_HEADS_DIL) for bk in _dilated_buckets())
    bias_diff = _bias_call(rel_bias, _diff_buckets(s, DIFF_TILE), N_HEADS_DIL, N_HEADS_DIFF)

    col = jnp.arange(w_in.shape[-1] - DIFF_WIDTH) // DIL_WIDTH
    qscale = jnp.where((col == 0) | (col == 3), LOG2E * HEAD_DIM ** -0.5, 1.0).astype(F32)[None]

    x2 = _permute_call(x.reshape(b * s, d), inverse=False)
    for l in range(depth):
        lam_init = 0.8 - 0.6 * math.exp(-0.3 * l)
        w_in_l = w_in[l].astype(BF16)
        n_tok = w_in_l.shape[1] - DIFF_WIDTH
        proj, vt = _inproj_call(x2, g_attn[l][None], w_in_l[:, :n_tok], w_in_l[:, n_tok:].T, qscale)
        proj = proj.reshape(b, s, -1)
        oa = _dil_call(proj, bias_a, bias_b, bias_c)
        ob = _diff_call(proj, vt, bias_diff, lambda_qk[l][None], subln_g[l][None, None],
                        lam_init=lam_init)
        x2 = _mlp_call(x2, oa.reshape(b * s, DIL_WIDTH), ob.reshape(b * s, DIFF_WIDTH),
                       w_out[l].astype(BF16), g_ffn[l][None],
                       w_gate_up[l].astype(BF16), w_down[l].astype(BF16),
                       g_final[None] if l == depth - 1 else None)
    return _permute_call(x2, inverse=True).reshape(b, s, d)
```

```python
import functools
import math

import jax
import jax.numpy as jnp
from jax import lax
from jax.experimental import pallas as pl
from jax.experimental.pallas import tpu as pltpu

F32 = jnp.float32
BF16 = jnp.bfloat16

HEAD_DIM = 64
N_HEADS_DIL = 8
N_HEADS_DIFF = 4
DIL_WIDTH = N_HEADS_DIL * HEAD_DIM
DIFF_WIDTH = N_HEADS_DIFF * 2 * HEAD_DIM
MIX_WIDTH = DIL_WIDTH + DIFF_WIDTH
BAND = 128
N_BUCKETS = 32
MAX_DISTANCE = 2048
EPS = 1e-6
SUBLN_EPS = 1e-5
NEG = -1e30
LOG2E = math.log2(math.e)

LANES = 128
VMEM_LIMIT = 56 * 1024 * 1024
ROW_TILE = 1024
TILE = 256
GROUP_ROWS = 16
PERM_ROWS = 8192
DIFF_TILE = TILE
FF_CHUNK = 768
ONES_ROWS = 16
DIFF_STREAMS = 2


def _nt_dot(a, b):
    return lax.dot_general(a, b, (((1,), (1,)), ((), ())), preferred_element_type=F32)


def _dot(a, b):
    return jnp.dot(a, b, preferred_element_type=F32)


def _rms(x, g, eps):
    return (x * lax.rsqrt(jnp.mean(x * x, axis=-1, keepdims=True) + eps)) * g


def _rel_bucket(dist):
    max_exact = N_BUCKETS // 2
    n = jnp.maximum(dist, 0)
    nf = jnp.maximum(n, 1).astype(F32)
    large = max_exact + (jnp.log(nf / max_exact) / math.log(MAX_DISTANCE / max_exact)
                         * (N_BUCKETS - max_exact)).astype(jnp.int32)
    large = jnp.minimum(large, N_BUCKETS - 1)
    return jnp.where(n < max_exact, n, large)


def _bias_kernel(tab_ref, bkt_ref, o_ref, *, head0):
    h = pl.program_id(0) + head0
    bkt = bkt_ref[0]
    out = jnp.full(bkt.shape, NEG, F32)
    for b in range(N_BUCKETS):
        out = jnp.where(bkt == b, tab_ref[b, h] * LOG2E, out)
    o_ref[0, 0] = out


def _bias_call(rel_bias, buckets, head0, n_heads):
    nt, r, c = buckets.shape
    return pl.pallas_call(
        functools.partial(_bias_kernel, head0=head0),
        grid=(n_heads, nt),
        in_specs=[pl.BlockSpec(memory_space=pltpu.SMEM),
                  pl.BlockSpec((1, r, c), lambda h, t: (t, 0, 0))],
        out_specs=pl.BlockSpec((1, 1, r, c), lambda h, t: (h, t, 0, 0)),
        out_shape=jax.ShapeDtypeStruct((n_heads, nt, r, c), F32),
        name="bias_tiles",
    )(rel_bias, buckets)


def _tile_token_offsets():
    pos = jnp.arange(TILE, dtype=jnp.int32)
    c, a, il = pos // 64, (pos // GROUP_ROWS) % 4, pos % GROUP_ROWS
    return GROUP_ROWS * il + 4 * a + c


def _band_buckets(mq, mk, dil, first):
    m = mq[:, None] - mk[None, :]
    valid = (m >= 0) & (m <= BAND)
    if first:
        valid = valid & (mk[None, :] >= 0)
    return jnp.where(valid, _rel_bucket(m * dil), -1)


def _dilated_buckets():
    half = TILE // 2
    r = jnp.arange(half, dtype=jnp.int32)
    g, il_lo = r // 8, r % 8
    w = GROUP_ROWS * il_lo + 4 * (g % 4) + g // 4
    mk = jnp.concatenate([w - half, w])
    tiles_a = jnp.stack([_band_buckets(w, mk, 1, f) for f in (False, True)])
    r = jnp.arange(64, dtype=jnp.int32)
    j = 4 * (r % GROUP_ROWS) + r // GROUP_ROWS
    mq = jnp.concatenate([j, j + 64])
    mk = jnp.concatenate([j - 128, j - 64, j, j + 64])
    tiles_b = jnp.stack([_band_buckets(mq, mk, 4, f) for f in (False, True)])
    i = jnp.arange(TILE, dtype=jnp.int32)
    tiles_c = _band_buckets(i, i, 16, False)[None]
    return tiles_a, tiles_b, tiles_c


def _diff_buckets(seq, tile):
    u = _tile_token_offsets()
    i = u[None, None, :]
    j = u[None, :, None]
    d = jnp.arange(seq // tile, dtype=jnp.int32)[:, None, None]
    dist = d * tile + i - j
    return jnp.where(dist >= 0, _rel_bucket(dist), -1)


def _permute_kernel(x_ref, o_ref, *, inverse):
    for tt in range(PERM_ROWS // TILE):
        for c in range(4):
            for a in range(4):
                nat = pl.ds(tt * TILE + 4 * a + c, GROUP_ROWS, stride=GROUP_ROWS)
                stored = pl.ds(tt * TILE + 64 * c + GROUP_ROWS * a, GROUP_ROWS)
                if inverse:
                    o_ref[nat, :] = x_ref[stored, :]
                else:
                    o_ref[stored, :] = x_ref[nat, :]


def _permute_call(x2, inverse):
    m, d = x2.shape
    return pl.pallas_call(
        functools.partial(_permute_kernel, inverse=inverse),
        grid=(m // PERM_ROWS, d // LANES),
        in_specs=[pl.BlockSpec((PERM_ROWS, LANES), lambda i, j: (i, j))],
        out_specs=pl.BlockSpec((PERM_ROWS, LANES), lambda i, j: (i, j)),
        out_shape=jax.ShapeDtypeStruct((m, d), x2.dtype),
        compiler_params=pltpu.CompilerParams(
            dimension_semantics=("parallel", "parallel"), vmem_limit_bytes=VMEM_LIMIT),
        name="permute",
    )(x2)


def _inproj_kernel(x_ref, g_ref, w_ref, wvt_ref, cs_ref, o_ref, vt_ref):
    h = _rms(x_ref[...], g_ref[...], EPS).astype(BF16)
    o_ref[...] = (_dot(h, w_ref[...]) * cs_ref[...]).astype(BF16)
    vt = _nt_dot(wvt_ref[...], h).astype(BF16)
    for j in range(vt_ref.shape[0]):
        vt_ref[j] = vt[:, j * DIFF_TILE:(j + 1) * DIFF_TILE]


def _inproj_call(x2, g, w, wvt, colscale):
    m, d = x2.shape
    n, f = w.shape[1], wvt.shape[0]
    per_step = ROW_TILE // DIFF_TILE
    out_specs = [pl.BlockSpec((ROW_TILE, n), lambda i: (i, 0)),
                 pl.BlockSpec((per_step, f, DIFF_TILE), lambda i: (i, 0, 0))]
    out_shape = [jax.ShapeDtypeStruct((m, n), BF16),
                 jax.ShapeDtypeStruct((m // DIFF_TILE, f, DIFF_TILE), BF16)]
    return pl.pallas_call(
        _inproj_kernel,
        grid=(m // ROW_TILE,),
        in_specs=[pl.BlockSpec((ROW_TILE, d), lambda i: (i, 0)),
                  pl.BlockSpec((1, d), lambda i: (0, 0)),
                  pl.BlockSpec((d, n), lambda i: (0, 0)),
                  pl.BlockSpec((f, d), lambda i: (0, 0)),
                  pl.BlockSpec((1, n), lambda i: (0, 0))],
        out_specs=out_specs,
        out_shape=out_shape,
        compiler_params=pltpu.CompilerParams(
            dimension_semantics=("parallel",), vmem_limit_bytes=VMEM_LIMIT),
        name="inproj",
    )(x2, g, w, wvt, colscale)


def _head_masks(shape, dtype):
    lane = lax.broadcasted_iota(jnp.int32, shape, 1)
    return [jnp.where(sel, 1.0, 0.0).astype(dtype) for sel in (lane < HEAD_DIM, lane >= HEAD_DIM)]


def _v_ext(v):
    rows = [jnp.concatenate([v * mk, mk], axis=-1) for mk in _head_masks(v.shape, v.dtype)]
    return jnp.concatenate(rows, axis=0)


def _attend(q, k, v, bias, m_old, accl_old):
    r, kk = q.shape[0], k.shape[0]
    ps, m_new = [], []
    for h, mk in enumerate(_head_masks(q.shape, q.dtype)):
        s = _nt_dot(q * mk, k) + bias[h]
        mx = jnp.broadcast_to(jnp.max(s, axis=-1, keepdims=True), (r, LANES))
        mn = mx if m_old is None else jnp.maximum(m_old[h], mx)
        ps.append(jnp.exp2(s - jnp.concatenate([mn] * (kk // LANES), axis=-1)).astype(BF16))
        m_new.append(mn)
    accl = _dot(jnp.concatenate(ps, axis=-1), _v_ext(v))
    if m_old is not None:
        first_head = lax.broadcasted_iota(jnp.int32, (r, LANES), 1) < HEAD_DIM
        alpha = jnp.where(first_head, jnp.exp2(m_old[0] - m_new[0]), jnp.exp2(m_old[1] - m_new[1]))
        accl = jnp.concatenate([alpha, alpha], axis=-1) * accl_old + accl
    return m_new, accl


def _dil_kernel(q_ref, k_ref, v_ref, ba_ref, bb_ref, bc_ref, o_ref, m0_ref, m1_ref, accl_ref):
    n_tiles = q_ref.shape[1] // TILE
    half = TILE // 2
    m_refs = (m0_ref, m1_ref)

    def halves(ref, n):
        x = ref[0, pl.ds(pl.multiple_of(n * TILE, TILE), TILE), :].astype(F32)
        x = x.reshape(TILE // GROUP_ROWS, 2, 8, LANES)
        return [x[:, hh].reshape(half, LANES).astype(BF16) for hh in (0, 1)]

    def whole(x0, x1):
        w = x0.shape[-1]
        parts = [x.reshape(TILE // GROUP_ROWS, 1, 8, w) for x in (x0, x1)]
        return jnp.concatenate(parts, axis=1).reshape(TILE, w)

    def pass_a(n, carry):
        qh, kh, vh = halves(q_ref, n), halves(k_ref, n), halves(v_ref, n)
        prev = jnp.maximum(n - 1, 0)
        k_prev, v_prev = halves(k_ref, prev)[1], halves(v_ref, prev)[1]
        first = jnp.where(n == 0, 1, 0)
        res = []
        for hh in (0, 1):
            variant = first if hh == 0 else 0
            res.append(_attend(qh[hh],
                               jnp.concatenate([k_prev if hh == 0 else kh[0], kh[hh]], axis=0),
                               jnp.concatenate([v_prev if hh == 0 else vh[0], vh[hh]], axis=0),
                               [ba_ref[h, variant] for h in (0, 1)], None, None))
        rows = pl.ds(pl.multiple_of(n * TILE, TILE), TILE)
        accl_ref[rows, :] = whole(res[0][1], res[1][1])
        for h in (0, 1):
            m_refs[h][rows, :] = whole(res[0][0][h], res[1][0][h])
        return carry

    lax.fori_loop(0, n_tiles, pass_a, 0, unroll=True)

    def pass_b(nb, carry):
        t0 = 2 * nb
        first = jnp.where(nb == 0, 1, 0)
        key_tiles = [jnp.maximum(t0 - 2, 0), jnp.maximum(t0 - 1, 0), t0, t0 + 1]
        for c in range(4):
            def chunks(ref, tiles, batch=None):
                parts = []
                for t in tiles:
                    rows = pl.ds(pl.multiple_of(t * TILE + c * 64, 64), 64)
                    parts.append(ref[rows, :] if batch is None else ref[batch, rows, :])
                return jnp.concatenate(parts, axis=0)
            m_new, accl = _attend(chunks(q_ref, [t0, t0 + 1], 0), chunks(k_ref, key_tiles, 0),
                                  chunks(v_ref, key_tiles, 0),
                                  [bb_ref[h, first] for h in (0, 1)],
                                  [chunks(m_refs[h], [t0, t0 + 1]) for h in (0, 1)],
                                  chunks(accl_ref, [t0, t0 + 1]))
            for i, t in enumerate((t0, t0 + 1)):
                rows = pl.ds(pl.multiple_of(t * TILE + c * 64, 64), 64)
                accl_ref[rows, :] = accl[64 * i:64 * (i + 1)]
                for h in (0, 1):
                    m_refs[h][rows, :] = m_new[h][64 * i:64 * (i + 1)]
        return carry

    lax.fori_loop(0, n_tiles // 2, pass_b, 0, unroll=True)

    def pass_c(a, carry):
        for c in range(4):
            def atoms(ref, batch=None):
                parts = []
                for t in range(n_tiles):
                    rows = pl.ds(pl.multiple_of(t * TILE + c * 64 + a * GROUP_ROWS, GROUP_ROWS),
                                 GROUP_ROWS)
                    parts.append(ref[rows, :] if batch is None else ref[batch, rows, :])
                return jnp.concatenate(parts, axis=0)
            _, accl = _attend(atoms(q_ref, 0), atoms(k_ref, 0), atoms(v_ref, 0),
                              [bc_ref[h, 0] for h in (0, 1)],
                              [atoms(m_refs[h]) for h in (0, 1)], atoms(accl_ref))
            out = (accl[:, :LANES] / accl[:, LANES:]).astype(BF16)
            for t in range(n_tiles):
                rows = pl.ds(pl.multiple_of(t * TILE + c * 64 + a * GROUP_ROWS, GROUP_ROWS),
                             GROUP_ROWS)
                o_ref[0, rows, :] = out[GROUP_ROWS * t:GROUP_ROWS * (t + 1)]
        return carry

    lax.fori_loop(0, 4, pass_c, 0, unroll=True)


def _dil_call(proj, bias_a, bias_b, bias_c):
    b, s, w = proj.shape
    assert s == TILE * GROUP_ROWS
    hw = 2 * HEAD_DIM
    n_pairs = N_HEADS_DIL // 2
    k0, v0 = DIL_WIDTH // hw, 2 * DIL_WIDTH // hw

    def bias_spec(bias):
        return pl.BlockSpec((2,) + bias.shape[1:], lambda hp, bi: (hp, 0, 0, 0))

    return pl.pallas_call(
        _dil_kernel,
        grid=(n_pairs, b),
        in_specs=[pl.BlockSpec((1, s, hw), lambda hp, bi: (bi, 0, hp)),
                  pl.BlockSpec((1, s, hw), lambda hp, bi: (bi, 0, k0 + hp)),
                  pl.BlockSpec((1, s, hw), lambda hp, bi: (bi, 0, v0 + hp)),
                  bias_spec(bias_a), bias_spec(bias_b), bias_spec(bias_c)],
        out_specs=pl.BlockSpec((1, s, hw), lambda hp, bi: (bi, 0, hp)),
        out_shape=jax.ShapeDtypeStruct((b, s, DIL_WIDTH), BF16),
        scratch_shapes=[pltpu.VMEM((s, LANES), F32),
                        pltpu.VMEM((s, LANES), F32),
                        pltpu.VMEM((s, 2 * LANES), F32)],
        compiler_params=pltpu.CompilerParams(
            dimension_semantics=("parallel", "parallel"), vmem_limit_bytes=VMEM_LIMIT),
        name="dilated_attn",
    )(proj, proj, proj, bias_a, bias_b, bias_c)


def _diff_kernel(lq_ref, q_ref, k_ref, vt_ref, bias_ref, g_ref, o_ref,
                 acc_ref, sa_ref, sb_ref, qq_ref, *, lam_init):
    t = DIFF_TILE
    hw = 2 * HEAD_DIM
    ones = jnp.ones((ONES_ROWS, t), BF16)
    lq = lq_ref[0]
    lam = (jnp.exp(jnp.sum(lq[0:1] * lq[1:2], axis=-1, keepdims=True))
           - jnp.exp(jnp.sum(lq[2:3] * lq[3:4], axis=-1, keepdims=True)) + lam_init)

    n_tiles = q_ref.shape[1] // t
    streams = range(q_ref.shape[0])

    def stage(ref, e, s):
        ref[e] = s
        return jnp.max(s, axis=0, keepdims=True)

    def load_queries(qi):
        rows = pl.ds(qi * t, t)
        bias0 = bias_ref[0, qi]
        bias0 = jnp.concatenate([bias0, bias0], axis=-1)
        mxa = []
        for e in streams:
            q = q_ref[e, rows, :]
            lane = lax.broadcasted_iota(jnp.int32, q.shape, 1)
            zero = jnp.zeros_like(q)
            qq = jnp.concatenate([jnp.where(lane < HEAD_DIM, q, zero),
                                  jnp.where(lane >= HEAD_DIM, q, zero)], axis=0)
            qq_ref[e] = qq
            mxa.append(stage(sa_ref, e, _nt_dot(k_ref[e, pl.ds(0, t), :], qq) + bias0))
        return mxa

    def q_block(qi, mxa):
        def scores(e, kj):
            off = pl.multiple_of(kj * t, t)
            bias = bias_ref[0, qi - kj]
            return (_nt_dot(k_ref[e, pl.ds(off, t), :], qq_ref[e])
                    + jnp.concatenate([bias, bias], axis=-1))

        def accumulate(e, s, mx, kj, m_prev):
            vt_ext = jnp.concatenate([vt_ref[e * n_tiles + kj], ones], axis=0)
            m_next = jnp.maximum(m_prev, mx)
            p = jnp.exp2(s - m_next).astype(BF16)
            acc_ref[e] = jnp.exp2(m_prev - m_next) * acc_ref[e] + _dot(vt_ext, p)
            return m_next

        def pair(k0, carry, prefetch):
            ms, mxa = carry
            mxb = [stage(sb_ref, e, scores(e, k0 + 1)) for e in streams]
            ms = [accumulate(e, sa_ref[e], mxa[e], k0, ms[e]) for e in streams]
            if prefetch:
                mxa = [stage(sa_ref, e, scores(e, k0 + 2)) for e in streams]
            return [accumulate(e, sb_ref[e], mxb[e], k0 + 1, ms[e]) for e in streams], mxa

        n_blocks = qi + 1
        odd = n_blocks % 2
        full = n_blocks // 2 if odd else n_blocks // 2 - 1
        carry = ([jnp.full((1, 2 * t), NEG, F32) for _ in streams], mxa)
        if full:
            carry = lax.fori_loop(0, full, lambda jj, c: pair(2 * jj, c, True), carry, unroll=2)
        if odd:
            ms, mxa = carry
            for e in streams:
                accumulate(e, sa_ref[e], mxa[e], qi, ms[e])
        else:
            pair(qi - 1, carry, False)

        rows = pl.ds(qi * t, t)
        for e in streams:
            acc = acc_ref[e]
            o = acc[:hw] / acc[hw:hw + 1]
            a = (o[:, :t] - lam * o[:, t:]).T
            o_ref[e, rows, :] = (_rms(a, g_ref[0], SUBLN_EPS) * (1.0 - lam_init)).astype(BF16)

    acc_ref[...] = jnp.zeros(acc_ref.shape, F32)
    for qi in range(n_tiles):
        q_block(qi, load_queries(qi))


def _diff_call(proj, vt, bias, lq, g, *, lam_init):
    b, s, w = proj.shape
    t = DIFF_TILE
    hw = 2 * HEAD_DIM
    q0 = 3 * DIL_WIDTH // hw
    k0 = q0 + N_HEADS_DIFF
    ns = DIFF_STREAMS
    assert b % ns == 0
    return pl.pallas_call(
        functools.partial(_diff_kernel, lam_init=lam_init),
        grid=(N_HEADS_DIFF, b // ns),
        in_specs=[pl.BlockSpec((1, 4, HEAD_DIM), lambda h, bi: (0, 0, 0)),
                  pl.BlockSpec((ns, s, hw), lambda h, bi: (bi, 0, q0 + h)),
                  pl.BlockSpec((ns, s, hw), lambda h, bi: (bi, 0, k0 + h)),
                  pl.BlockSpec((ns * (s // t), hw, t), lambda h, bi: (bi, h, 0)),
                  pl.BlockSpec((1, s // t, t, t), lambda h, bi: (h, 0, 0, 0)),
                  pl.BlockSpec((1, 1, hw), lambda h, bi: (0, 0, 0))],
        out_specs=pl.BlockSpec((ns, s, hw), lambda h, bi: (bi, 0, h)),
        out_shape=jax.ShapeDtypeStruct((b, s, DIFF_WIDTH), BF16),
        scratch_shapes=[pltpu.VMEM((ns, hw + ONES_ROWS, 2 * t), F32),
                        pltpu.VMEM((ns, t, 2 * t), F32),
                        pltpu.VMEM((ns, t, 2 * t), F32),
                        pltpu.VMEM((ns, 2 * t, hw), BF16)],
        compiler_params=pltpu.CompilerParams(
            dimension_semantics=("parallel", "parallel"), vmem_limit_bytes=VMEM_LIMIT),
        name="diff_attn",
    )(lq, proj, proj, vt, bias, g)


def _ff_chunks(d_ff):
    chunks, c0 = [], 0
    while c0 < d_ff:
        cw = min(FF_CHUNK, d_ff - c0)
        chunks.append((c0, cw))
        c0 += cw
    return chunks


def _mlp_kernel(*refs, final):
    x_ref, oa_ref, ob_ref, wo_ref, g_ref, wgu_ref, wd_ref = refs[:7]
    o_ref, act_ref = refs[-2:]
    d_ff = wd_ref.shape[0]
    x = (x_ref[...] + _dot(oa_ref[...], wo_ref[:DIL_WIDTH, :])
         + _dot(ob_ref[...], wo_ref[DIL_WIDTH:, :]))
    h = _rms(x, g_ref[...], EPS).astype(BF16)
    for c0, cw in _ff_chunks(d_ff):
        gate = _dot(h, wgu_ref[:, c0:c0 + cw])
        up = _dot(h, wgu_ref[:, d_ff + c0:d_ff + c0 + cw])
        act_ref[:, c0:c0 + cw] = ((gate * jax.nn.sigmoid(gate)) * up).astype(BF16)
    x = x + _dot(act_ref[...], wd_ref[...])
    if final:
        x = _rms(x, refs[7][...], EPS)
    o_ref[...] = x


def _mlp_call(x2, oa, ob, wo, g, wgu, wd, g_final):
    m, d = x2.shape
    d_ff = wd.shape[0]
    final = g_final is not None
    row = lambda i: (i, 0)
    fixed = lambda i: (0, 0)
    in_specs = [pl.BlockSpec((ROW_TILE, d), row),
                pl.BlockSpec((ROW_TILE, DIL_WIDTH), row),
                pl.BlockSpec((ROW_TILE, DIFF_WIDTH), row),
                pl.BlockSpec((MIX_WIDTH, d), fixed, pipeline_mode=pl.Buffered(1)),
                pl.BlockSpec((1, d), fixed),
                pl.BlockSpec((d, 2 * d_ff), fixed, pipeline_mode=pl.Buffered(1)),
                pl.BlockSpec((d_ff, d), fixed, pipeline_mode=pl.Buffered(1))]
    args = [x2, oa, ob, wo, g, wgu, wd]
    if final:
        in_specs.append(pl.BlockSpec((1, d), fixed))
        args.append(g_final)
    return pl.pallas_call(
        functools.partial(_mlp_kernel, final=final),
        grid=(m // ROW_TILE,),
        in_specs=in_specs,
        out_specs=pl.BlockSpec((ROW_TILE, d), row),
        out_shape=jax.ShapeDtypeStruct((m, d), F32),
        scratch_shapes=[pltpu.VMEM((ROW_TILE, d_ff), BF16)],
        compiler_params=pltpu.CompilerParams(
            dimension_semantics=("parallel",), vmem_limit_bytes=VMEM_LIMIT),
        name="outproj_mlp",
    )(*args)


def kernel(x, g_attn, w_in, w_out, rel_bias, lambda_qk, subln_g, g_ffn,
           w_gate_up, w_down, g_final):
    b, s, d = x.shape
    depth = w_in.shape[0]
    assert (b * s) % PERM_ROWS == 0 and PERM_ROWS % ROW_TILE == 0 and d % LANES == 0

    bias_a, bias_b, bias_c = (_bias_call(rel_bias, bk, 0, N_HEADS_DIL) for bk in _dilated_buckets())
    bias_diff = _bias_call(rel_bias, _diff_buckets(s, DIFF_TILE), N_HEADS_DIL, N_HEADS_DIFF)

    col = jnp.arange(w_in.shape[-1] - DIFF_WIDTH) // DIL_WIDTH
    qscale = jnp.where((col == 0) | (col == 3), LOG2E * HEAD_DIM ** -0.5, 1.0).astype(F32)[None]

    x2 = _permute_call(x.reshape(b * s, d), inverse=False)
    for l in range(depth):
        lam_init = 0.8 - 0.6 * math.exp(-0.3 * l)
        w_in_l = w_in[l].astype(BF16)
        n_tok = w_in_l.shape[1] - DIFF_WIDTH
        proj, vt = _inproj_call(x2, g_attn[l][None], w_in_l[:, :n_tok], w_in_l[:, n_tok:].T, qscale)
        proj = proj.reshape(b, s, -1)
        oa = _dil_call(proj, bias_a, bias_b, bias_c)
        ob = _diff_call(proj, vt, bias_diff, lambda_qk[l][None], subln_g[l][None, None],
                        lam_init=lam_init)
        x2 = _mlp_call(x2, oa.reshape(b * s, DIL_WIDTH), ob.reshape(b * s, DIFF_WIDTH),
                       w_out[l].astype(BF16), g_ffn[l][None],
                       w_gate_up[l].astype(BF16), w_down[l].astype(BF16),
                       g_final[None] if l == depth - 1 else None)
    return _permute_call(x2, inverse=True).reshape(b, s, d)
```

```python
import functools
import math

import jax
import jax.numpy as jnp
from jax import lax
from jax.experimental import pallas as pl
from jax.experimental.pallas import tpu as pltpu

F32 = jnp.float32
BF16 = jnp.bfloat16

HEAD_DIM = 64
N_HEADS_DIL = 8
N_HEADS_DIFF = 4
DIL_WIDTH = N_HEADS_DIL * HEAD_DIM
DIFF_WIDTH = N_HEADS_DIFF * 2 * HEAD_DIM
MIX_WIDTH = DIL_WIDTH + DIFF_WIDTH
BAND = 128
N_BUCKETS = 32
MAX_DISTANCE = 2048
EPS = 1e-6
SUBLN_EPS = 1e-5
NEG = -1e30
LOG2E = math.log2(math.e)

LANES = 128
VMEM_LIMIT = 56 * 1024 * 1024
ROW_TILE = 1024
TILE = 256
GROUP_ROWS = 16
PERM_ROWS = 8192
DIFF_TILE = TILE
FF_CHUNK = 768
ONES_ROWS = 16
DIFF_STREAMS = 2


def _nt_dot(a, b):
    return lax.dot_general(a, b, (((1,), (1,)), ((), ())), preferred_element_type=F32)


def _dot(a, b):
    return jnp.dot(a, b, preferred_element_type=F32)


def _rms(x, g, eps):
    return (x * lax.rsqrt(jnp.mean(x * x, axis=-1, keepdims=True) + eps)) * g


def _rel_bucket(dist):
    max_exact = N_BUCKETS // 2
    n = jnp.maximum(dist, 0)
    nf = jnp.maximum(n, 1).astype(F32)
    large = max_exact + (jnp.log(nf / max_exact) / math.log(MAX_DISTANCE / max_exact)
                         * (N_BUCKETS - max_exact)).astype(jnp.int32)
    large = jnp.minimum(large, N_BUCKETS - 1)
    return jnp.where(n < max_exact, n, large)


def _bias_kernel(tab_ref, bkt_ref, o_ref, *, head0):
    h = pl.program_id(0) + head0
    bkt = bkt_ref[0]
    out = jnp.full(bkt.shape, NEG, F32)
    for b in range(N_BUCKETS):
        out = jnp.where(bkt == b, tab_ref[b, h] * LOG2E, out)
    o_ref[0, 0] = out


def _bias_call(rel_bias, buckets, head0, n_heads):
    nt, r, c = buckets.shape
    return pl.pallas_call(
        functools.partial(_bias_kernel, head0=head0),
        grid=(n_heads, nt),
        in_specs=[pl.BlockSpec(memory_space=pltpu.SMEM),
                  pl.BlockSpec((1, r, c), lambda h, t: (t, 0, 0))],
        out_specs=pl.BlockSpec((1, 1, r, c), lambda h, t: (h, t, 0, 0)),
        out_shape=jax.ShapeDtypeStruct((n_heads, nt, r, c), F32),
        name="bias_tiles",
    )(rel_bias, buckets)


def _tile_token_offsets():
    pos = jnp.arange(TILE, dtype=jnp.int32)
    c, a, il = pos // 64, (pos // GROUP_ROWS) % 4, pos % GROUP_ROWS
    return GROUP_ROWS * il + 4 * a + c


def _band_buckets(mq, mk, dil, first):
    m = mq[:, None] - mk[None, :]
    valid = (m >= 0) & (m <= BAND)
    if first:
        valid = valid & (mk[None, :] >= 0)
    return jnp.where(valid, _rel_bucket(m * dil), -1)


def _dilated_buckets():
    half = TILE // 2
    r = jnp.arange(half, dtype=jnp.int32)
    g, il_lo = r // 8, r % 8
    w = GROUP_ROWS * il_lo + 4 * (g % 4) + g // 4
    mk = jnp.concatenate([w - half, w])
    tiles_a = jnp.stack([_band_buckets(w, mk, 1, f) for f in (False, True)])
    r = jnp.arange(64, dtype=jnp.int32)
    j = 4 * (r % GROUP_ROWS) + r // GROUP_ROWS
    mq = jnp.concatenate([j, j + 64])
    mk = jnp.concatenate([j - 128, j - 64, j, j + 64])
    tiles_b = jnp.stack([_band_buckets(mq, mk, 4, f) for f in (False, True)])
    i = jnp.arange(TILE, dtype=jnp.int32)
    tiles_c = _band_buckets(i, i, 16, False)[None]
    return tiles_a, tiles_b, tiles_c


def _diff_buckets(seq, tile):
    u = _tile_token_offsets()
    i = u[None, None, :]
    j = u[None, :, None]
    d = jnp.arange(seq // tile, dtype=jnp.int32)[:, None, None]
    dist = d * tile + i - j
    return jnp.where(dist >= 0, _rel_bucket(dist), -1)


def _permute_kernel(x_ref, o_ref, *, inverse):
    for tt in range(PERM_ROWS // TILE):
        for c in range(4):
            for a in range(4):
                nat = pl.ds(tt * TILE + 4 * a + c, GROUP_ROWS, stride=GROUP_ROWS)
                stored = pl.ds(tt * TILE + 64 * c + GROUP_ROWS * a, GROUP_ROWS)
                if inverse:
                    o_ref[nat, :] = x_ref[stored, :]
                else:
                    o_ref[stored, :] = x_ref[nat, :]


def _permute_call(x2, inverse):
    m, d = x2.shape
    return pl.pallas_call(
        functools.partial(_permute_kernel, inverse=inverse),
        grid=(m // PERM_ROWS, d // LANES),
        in_specs=[pl.BlockSpec((PERM_ROWS, LANES), lambda i, j: (i, j))],
        out_specs=pl.BlockSpec((PERM_ROWS, LANES), lambda i, j: (i, j)),
        out_shape=jax.ShapeDtypeStruct((m, d), x2.dtype),
        compiler_params=pltpu.CompilerParams(
            dimension_semantics=("parallel", "parallel"), vmem_limit_bytes=VMEM_LIMIT),
        name="permute",
    )(x2)


def _inproj_kernel(x_ref, g_ref, w_ref, wvt_ref, cs_ref, o_ref, vt_ref):
    h = _rms(x_ref[...], g_ref[...], EPS).astype(BF16)
    o_ref[...] = (_dot(h, w_ref[...]) * cs_ref[...]).astype(BF16)
    vt = _nt_dot(wvt_ref[...], h).astype(BF16)
    for j in range(vt_ref.shape[0]):
        vt_ref[j] = vt[:, j * DIFF_TILE:(j + 1) * DIFF_TILE]


def _inproj_call(x2, g, w, wvt, colscale):
    m, d = x2.shape
    n, f = w.shape[1], wvt.shape[0]
    per_step = ROW_TILE // DIFF_TILE
    out_specs = [pl.BlockSpec((ROW_TILE, n), lambda i: (i, 0)),
                 pl.BlockSpec((per_step, f, DIFF_TILE), lambda i: (i, 0, 0))]
    out_shape = [jax.ShapeDtypeStruct((m, n), BF16),
                 jax.ShapeDtypeStruct((m // DIFF_TILE, f, DIFF_TILE), BF16)]
    return pl.pallas_call(
        _inproj_kernel,
        grid=(m // ROW_TILE,),
        in_specs=[pl.BlockSpec((ROW_TILE, d), lambda i: (i, 0)),
                  pl.BlockSpec((1, d), lambda i: (0, 0)),
                  pl.BlockSpec((d, n), lambda i: (0, 0)),
                  pl.BlockSpec((f, d), lambda i: (0, 0)),
                  pl.BlockSpec((1, n), lambda i: (0, 0))],
        out_specs=out_specs,
        out_shape=out_shape,
        compiler_params=pltpu.CompilerParams(
            dimension_semantics=("parallel",), vmem_limit_bytes=VMEM_LIMIT),
        name="inproj",
    )(x2, g, w, wvt, colscale)


def _head_masks(shape, dtype):
    lane = lax.broadcasted_iota(jnp.int32, shape, 1)
    return [jnp.where(sel, 1.0, 0.0).astype(dtype) for sel in (lane < HEAD_DIM, lane >= HEAD_DIM)]


def _v_ext(v):
    rows = [jnp.concatenate([v * mk, mk], axis=-1) for mk in _head_masks(v.shape, v.dtype)]
    return jnp.concatenate(rows, axis=0)


def _attend(q, k, v, bias, m_old, accl_old):
    r, kk = q.shape[0], k.shape[0]
    ps, m_new = [], []
    for h, mk in enumerate(_head_masks(q.shape, q.dtype)):
        s = _nt_dot(q * mk, k) + bias[h]
        mx = jnp.broadcast_to(jnp.max(s, axis=-1, keepdims=True), (r, LANES))
        mn = mx if m_old is None else jnp.maximum(m_old[h], mx)
        ps.append(jnp.exp2(s - jnp.concatenate([mn] * (kk // LANES), axis=-1)).astype(BF16))
        m_new.append(mn)
    accl = _dot(jnp.concatenate(ps, axis=-1), _v_ext(v))
    if m_old is not None:
        first_head = lax.broadcasted_iota(jnp.int32, (r, LANES), 1) < HEAD_DIM
        alpha = jnp.where(first_head, jnp.exp2(m_old[0] - m_new[0]), jnp.exp2(m_old[1] - m_new[1]))
        accl = jnp.concatenate([alpha, alpha], axis=-1) * accl_old + accl
    return m_new, accl


def _dil_kernel(q_ref, k_ref, v_ref, ba_ref, bb_ref, bc_ref, o_ref, m0_ref, m1_ref, accl_ref):
    n_tiles = q_ref.shape[1] // TILE
    half = TILE // 2
    m_refs = (m0_ref, m1_ref)

    def halves(ref, n):
        x = ref[0, pl.ds(pl.multiple_of(n * TILE, TILE), TILE), :].astype(F32)
        x = x.reshape(TILE // GROUP_ROWS, 2, 8, LANES)
        return [x[:, hh].reshape(half, LANES).astype(BF16) for hh in (0, 1)]

    def whole(x0, x1):
        w = x0.shape[-1]
        parts = [x.reshape(TILE // GROUP_ROWS, 1, 8, w) for x in (x0, x1)]
        return jnp.concatenate(parts, axis=1).reshape(TILE, w)

    def pass_a(n, carry):
        qh, kh, vh = halves(q_ref, n), halves(k_ref, n), halves(v_ref, n)
        prev = jnp.maximum(n - 1, 0)
        k_prev, v_prev = halves(k_ref, prev)[1], halves(v_ref, prev)[1]
        first = jnp.where(n == 0, 1, 0)
        res = []
        for hh in (0, 1):
            variant = first if hh == 0 else 0
            res.append(_attend(qh[hh],
                               jnp.concatenate([k_prev if hh == 0 else kh[0], kh[hh]], axis=0),
                               jnp.concatenate([v_prev if hh == 0 else vh[0], vh[hh]], axis=0),
                               [ba_ref[h, variant] for h in (0, 1)], None, None))
        rows = pl.ds(pl.multiple_of(n * TILE, TILE), TILE)
        accl_ref[rows, :] = whole(res[0][1], res[1][1])
        for h in (0, 1):
            m_refs[h][rows, :] = whole(res[0][0][h], res[1][0][h])
        return carry

    lax.fori_loop(0, n_tiles, pass_a, 0, unroll=True)

    def pass_b(nb, carry):
        t0 = 2 * nb
        first = jnp.where(nb == 0, 1, 0)
        key_tiles = [jnp.maximum(t0 - 2, 0), jnp.maximum(t0 - 1, 0), t0, t0 + 1]
        for c in range(4):
            def chunks(ref, tiles, batch=None):
                parts = []
                for t in tiles:
                    rows = pl.ds(pl.multiple_of(t * TILE + c * 64, 64), 64)
                    parts.append(ref[rows, :] if batch is None else ref[batch, rows, :])
                return jnp.concatenate(parts, axis=0)
            m_new, accl = _attend(chunks(q_ref, [t0, t0 + 1], 0), chunks(k_ref, key_tiles, 0),
                                  chunks(v_ref, key_tiles, 0),
                                  [bb_ref[h, first] for h in (0, 1)],
                                  [chunks(m_refs[h], [t0, t0 + 1]) for h in (0, 1)],
                                  chunks(accl_ref, [t0, t0 + 1]))
            for i, t in enumerate((t0, t0 + 1)):
                rows = pl.ds(pl.multiple_of(t * TILE + c * 64, 64), 64)
                accl_ref[rows, :] = accl[64 * i:64 * (i + 1)]
                for h in (0, 1):
                    m_refs[h][rows, :] = m_new[h][64 * i:64 * (i + 1)]
        return carry

    lax.fori_loop(0, n_tiles // 2, pass_b, 0, unroll=True)

    def pass_c(a, carry):
        for c in range(4):
            def atoms(ref, batch=None):
                parts = []
                for t in range(n_tiles):
                    rows = pl.ds(pl.multiple_of(t * TILE + c * 64 + a * GROUP_ROWS, GROUP_ROWS),
                                 GROUP_ROWS)
                    parts.append(ref[rows, :] if batch is None else ref[batch, rows, :])
                return jnp.concatenate(parts, axis=0)
            _, accl = _attend(atoms(q_ref, 0), atoms(k_ref, 0), atoms(v_ref, 0),
                              [bc_ref[h, 0] for h in (0, 1)],
                              [atoms(m_refs[h]) for h in (0, 1)], atoms(accl_ref))
            out = (accl[:, :LANES] / accl[:, LANES:]).astype(BF16)
            for t in range(n_tiles):
                rows = pl.ds(pl.multiple_of(t * TILE + c * 64 + a * GROUP_ROWS, GROUP_ROWS),
                             GROUP_ROWS)
                o_ref[0, rows, :] = out[GROUP_ROWS * t:GROUP_ROWS * (t + 1)]
        return carry

    lax.fori_loop(0, 4, pass_c, 0, unroll=True)


def _dil_call(proj, bias_a, bias_b, bias_c):
    b, s, w = proj.shape
    assert s == TILE * GROUP_ROWS
    hw = 2 * HEAD_DIM
    n_pairs = N_HEADS_DIL // 2
    k0, v0 = DIL_WIDTH // hw, 2 * DIL_WIDTH // hw

    def bias_spec(bias):
        return pl.BlockSpec((2,) + bias.shape[1:], lambda hp, bi: (hp, 0, 0, 0))

    return pl.pallas_call(
        _dil_kernel,
        grid=(n_pairs, b),
        in_specs=[pl.BlockSpec((1, s, hw), lambda hp, bi: (bi, 0, hp)),
                  pl.BlockSpec((1, s, hw), lambda hp, bi: (bi, 0, k0 + hp)),
                  pl.BlockSpec((1, s, hw), lambda hp, bi: (bi, 0, v0 + hp)),
                  bias_spec(bias_a), bias_spec(bias_b), bias_spec(bias_c)],
        out_specs=pl.BlockSpec((1, s, hw), lambda hp, bi: (bi, 0, hp)),
        out_shape=jax.ShapeDtypeStruct((b, s, DIL_WIDTH), BF16),
        scratch_shapes=[pltpu.VMEM((s, LANES), F32),
                        pltpu.VMEM((s, LANES), F32),
                        pltpu.VMEM((s, 2 * LANES), F32)],
        compiler_params=pltpu.CompilerParams(
            dimension_semantics=("parallel", "parallel"), vmem_limit_bytes=VMEM_LIMIT),
        name="dilated_attn",
    )(proj, proj, proj, bias_a, bias_b, bias_c)


def _diff_kernel(lq_ref, q_ref, k_ref, vt_ref, bias_ref, g_ref, o_ref,
                 acc_ref, sa_ref, sb_ref, qq_ref, *, lam_init):
    t = DIFF_TILE
    hw = 2 * HEAD_DIM
    ones = jnp.ones((ONES_ROWS, t), BF16)
    lq = lq_ref[0]
    lam = (jnp.exp(jnp.sum(lq[0:1] * lq[1:2], axis=-1, keepdims=True))
           - jnp.exp(jnp.sum(lq[2:3] * lq[3:4], axis=-1, keepdims=True)) + lam_init)

    n_tiles = q_ref.shape[1] // t
    streams = range(q_ref.shape[0])

    chains = [(e, c) for e in streams for c in (0, 1)]

    def cols(c):
        return slice(c * t, (c + 1) * t)

    def stage(ref, ch, s):
        e, c = ch
        ref[e, :, cols(c)] = s
        return jnp.max(s, axis=0, keepdims=True)

    def load_queries(qi):
        rows = pl.ds(qi * t, t)
        bias0 = bias_ref[0, qi]
        mxa = []
        for e in streams:
            q = q_ref[e, rows, :]
            lane = lax.broadcasted_iota(jnp.int32, q.shape, 1)
            zero = jnp.zeros_like(q)
            for c, sel in enumerate((lane < HEAD_DIM, lane >= HEAD_DIM)):
                qm = jnp.where(sel, q, zero)
                qq_ref[e, c * t:(c + 1) * t, :] = qm
                mxa.append(stage(sa_ref, (e, c), _nt_dot(k_ref[e, pl.ds(0, t), :], qm) + bias0))
        return mxa

    def q_block(qi, mxa):
        def scores(ch, kj):
            e, c = ch
            off = pl.multiple_of(kj * t, t)
            return (_nt_dot(k_ref[e, pl.ds(off, t), :], qq_ref[e, c * t:(c + 1) * t, :])
                    + bias_ref[0, qi - kj])

        def accumulate(ch, ref, mx, kj, m_prev):
            e, c = ch
            vt_ext = jnp.concatenate([vt_ref[e * n_tiles + kj], ones], axis=0)
            m_next = jnp.maximum(m_prev, mx)
            p = jnp.exp2(ref[e, :, cols(c)] - m_next).astype(BF16)
            acc_ref[e, :, cols(c)] = (jnp.exp2(m_prev - m_next) * acc_ref[e, :, cols(c)]
                                      + _dot(vt_ext, p))
            return m_next

        def pair(k0, carry, prefetch):
            ms, mxa = carry
            mxb = [stage(sb_ref, ch, scores(ch, k0 + 1)) for ch in chains]
            ms = [accumulate(ch, sa_ref, mxa[n], k0, ms[n]) for n, ch in enumerate(chains)]
            if prefetch:
                mxa = [stage(sa_ref, ch, scores(ch, k0 + 2)) for ch in chains]
            return [accumulate(ch, sb_ref, mxb[n], k0 + 1, ms[n]) for n, ch in enumerate(chains)], mxa

        n_blocks = qi + 1
        odd = n_blocks % 2
        full = n_blocks // 2 if odd else n_blocks // 2 - 1
        carry = ([jnp.full((1, t), NEG, F32) for _ in chains], mxa)
        if full:
            carry = lax.fori_loop(0, full, lambda jj, c: pair(2 * jj, c, True), carry, unroll=2)
        if odd:
            ms, mxa = carry
            for n, ch in enumerate(chains):
                accumulate(ch, sa_ref, mxa[n], qi, ms[n])
        else:
            pair(qi - 1, carry, False)

        rows = pl.ds(qi * t, t)
        for e in streams:
            acc = acc_ref[e]
            o = acc[:hw] / acc[hw:hw + 1]
            a = (o[:, :t] - lam * o[:, t:]).T
            o_ref[e, rows, :] = (_rms(a, g_ref[0], SUBLN_EPS) * (1.0 - lam_init)).astype(BF16)

    acc_ref[...] = jnp.zeros(acc_ref.shape, F32)
    for qi in range(n_tiles):
        q_block(qi, load_queries(qi))


def _diff_call(proj, vt, bias, lq, g, *, lam_init):
    b, s, w = proj.shape
    t = DIFF_TILE
    hw = 2 * HEAD_DIM
    q0 = 3 * DIL_WIDTH // hw
    k0 = q0 + N_HEADS_DIFF
    ns = DIFF_STREAMS
    assert b % ns == 0
    return pl.pallas_call(
        functools.partial(_diff_kernel, lam_init=lam_init),
        grid=(N_HEADS_DIFF, b // ns),
        in_specs=[pl.BlockSpec((1, 4, HEAD_DIM), lambda h, bi: (0, 0, 0)),
                  pl.BlockSpec((ns, s, hw), lambda h, bi: (bi, 0, q0 + h)),
                  pl.BlockSpec((ns, s, hw), lambda h, bi: (bi, 0, k0 + h)),
                  pl.BlockSpec((ns * (s // t), hw, t), lambda h, bi: (bi, h, 0)),
                  pl.BlockSpec((1, s // t, t, t), lambda h, bi: (h, 0, 0, 0)),
                  pl.BlockSpec((1, 1, hw), lambda h, bi: (0, 0, 0))],
        out_specs=pl.BlockSpec((ns, s, hw), lambda h, bi: (bi, 0, h)),
        out_shape=jax.ShapeDtypeStruct((b, s, DIFF_WIDTH), BF16),
        scratch_shapes=[pltpu.VMEM((ns, hw + ONES_ROWS, 2 * t), F32),
                        pltpu.VMEM((ns, t, 2 * t), F32),
                        pltpu.VMEM((ns, t, 2 * t), F32),
                        pltpu.VMEM((ns, 2 * t, hw), BF16)],
        compiler_params=pltpu.CompilerParams(
            dimension_semantics=("parallel", "parallel"), vmem_limit_bytes=VMEM_LIMIT),
        name="diff_attn",
    )(lq, proj, proj, vt, bias, g)


def _ff_chunks(d_ff):
    chunks, c0 = [], 0
    while c0 < d_ff:
        cw = min(FF_CHUNK, d_ff - c0)
        chunks.append((c0, cw))
        c0 += cw
    return chunks


def _mlp_kernel(*refs, final):
    x_ref, oa_ref, ob_ref, wo_ref, g_ref, wgu_ref, wd_ref = refs[:7]
    o_ref, act_ref = refs[-2:]
    d_ff = wd_ref.shape[0]
    x = (x_ref[...] + _dot(oa_ref[...], wo_ref[:DIL_WIDTH, :])
         + _dot(ob_ref[...], wo_ref[DIL_WIDTH:, :]))
    h = _rms(x, g_ref[...], EPS).astype(BF16)
    for c0, cw in _ff_chunks(d_ff):
        gate = _dot(h, wgu_ref[:, c0:c0 + cw])
        up = _dot(h, wgu_ref[:, d_ff + c0:d_ff + c0 + cw])
        act_ref[:, c0:c0 + cw] = ((gate * jax.nn.sigmoid(gate)) * up).astype(BF16)
    x = x + _dot(act_ref[...], wd_ref[...])
    if final:
        x = _rms(x, refs[7][...], EPS)
    o_ref[...] = x


def _mlp_call(x2, oa, ob, wo, g, wgu, wd, g_final):
    m, d = x2.shape
    d_ff = wd.shape[0]
    final = g_final is not None
    row = lambda i: (i, 0)
    fixed = lambda i: (0, 0)
    in_specs = [pl.BlockSpec((ROW_TILE, d), row),
                pl.BlockSpec((ROW_TILE, DIL_WIDTH), row),
                pl.BlockSpec((ROW_TILE, DIFF_WIDTH), row),
                pl.BlockSpec((MIX_WIDTH, d), fixed, pipeline_mode=pl.Buffered(1)),
                pl.BlockSpec((1, d), fixed),
                pl.BlockSpec((d, 2 * d_ff), fixed, pipeline_mode=pl.Buffered(1)),
                pl.BlockSpec((d_ff, d), fixed, pipeline_mode=pl.Buffered(1))]
    args = [x2, oa, ob, wo, g, wgu, wd]
    if final:
        in_specs.append(pl.BlockSpec((1, d), fixed))
        args.append(g_final)
    return pl.pallas_call(
        functools.partial(_mlp_kernel, final=final),
        grid=(m // ROW_TILE,),
        in_specs=in_specs,
        out_specs=pl.BlockSpec((ROW_TILE, d), row),
        out_shape=jax.ShapeDtypeStruct((m, d), F32),
        scratch_shapes=[pltpu.VMEM((ROW_TILE, d_ff), BF16)],
        compiler_params=pltpu.CompilerParams(
            dimension_semantics=("parallel",), vmem_limit_bytes=VMEM_LIMIT),
        name="outproj_mlp",
    )(*args)


def kernel(x, g_attn, w_in, w_out, rel_bias, lambda_qk, subln_g, g_ffn,
           w_gate_up, w_down, g_final):
    b, s, d = x.shape
    depth = w_in.shape[0]
    assert (b * s) % PERM_ROWS == 0 and PERM_ROWS % ROW_TILE == 0 and d % LANES == 0

    bias_a, bias_b, bias_c = (_bias_call(rel_bias, bk, 0, N_HEADS_DIL) for bk in _dilated_buckets())
    bias_diff = _bias_call(rel_bias, _diff_buckets(s, DIFF_TILE), N_HEADS_DIL, N_HEADS_DIFF)

    col = jnp.arange(w_in.shape[-1] - DIFF_WIDTH) // DIL_WIDTH
    qscale = jnp.where((col == 0) | (col == 3), LOG2E * HEAD_DIM ** -0.5, 1.0).astype(F32)[None]

    x2 = _permute_call(x.reshape(b * s, d), inverse=False)
    for l in range(depth):
        lam_init = 0.8 - 0.6 * math.exp(-0.3 * l)
        w_in_l = w_in[l].astype(BF16)
        n_tok = w_in_l.shape[1] - DIFF_WIDTH
        proj, vt = _inproj_call(x2, g_attn[l][None], w_in_l[:, :n_tok], w_in_l[:, n_tok:].T, qscale)
        proj = proj.reshape(b, s, -1)
        oa = _dil_call(proj, bias_a, bias_b, bias_c)
        ob = _diff_call(proj, vt, bias_diff, lambda_qk[l][None], subln_g[l][None, None],
                        lam_init=lam_init)
        x2 = _mlp_call(x2, oa.reshape(b * s, DIL_WIDTH), ob.reshape(b * s, DIFF_WIDTH),
                       w_out[l].astype(BF16), g_ffn[l][None],
                       w_gate_up[l].astype(BF16), w_down[l].astype(BF16),
                       g_final[None] if l == depth - 1 else None)
    return _permute_call(x2, inverse=True).reshape(b, s, d)
```
